```python
import jax, jax.numpy as jnp
from jax import lax
import numpy as np

D_MODEL = 2048
BATCH = 8
SEQ = 2048
DEPTH = 1

CHUNK = 64
D_MIX = D_MODEL
D_A = D_MIX // 2
D_B = D_MIX - D_A
GROUP_DIM = 64
N_GROUPS_A = D_A // GROUP_DIM
N_GROUPS_B = D_B // GROUP_DIM
CONV_A_WIDTH = 3
CONV_B_WIDTH = 31
D_IN = 3 * D_A + 2 * D_B
D_FF = 4 * D_MODEL
EPS = 1e-6

kernel_name = "hybrid_shortconv_conformer_conv_block"


def rms_norm(x, gain):
    xf = x.astype(jnp.float32)
    y = xf * lax.rsqrt(jnp.mean(xf * xf, axis=-1, keepdims=True) + EPS)
    return (y * gain.astype(jnp.float32)).astype(x.dtype)


def layer_norm(x, gain, bias):
    xf = x.astype(jnp.float32)
    mu = jnp.mean(xf, axis=-1, keepdims=True)
    var = jnp.mean(jnp.square(xf - mu), axis=-1, keepdims=True)
    y = (xf - mu) * lax.rsqrt(var + EPS)
    return (y * gain.astype(jnp.float32) + bias.astype(jnp.float32)).astype(x.dtype)


def causal_dwconv(u, w):
    k, c = w.shape
    return lax.conv_general_dilated(
        u, w[:, None, :].astype(u.dtype),
        window_strides=(1,),
        padding=[(k - 1, 0)],
        dimension_numbers=("NWC", "WIO", "NWC"),
        feature_group_count=c,
    )


def setup_inputs(seed: int = 0) -> dict:
    key = jax.random.key(seed)
    ks = jax.random.split(key, 16)
    f32 = jnp.float32

    def normal(k, shape, scale):
        return jax.random.normal(k, shape, f32) * scale

    def gain(k, n):
        return jnp.ones((DEPTH, n), f32) + normal(k, (DEPTH, n), 0.02)

    return {
        "x": normal(ks[0], (BATCH, SEQ, D_MODEL), 1.0),
        "mix_pre_gain": gain(ks[1], D_MODEL),
        "w_in": normal(ks[2], (DEPTH, D_MODEL, D_IN), D_MODEL ** -0.5),
        "conv_a_w": normal(ks[3], (DEPTH, CONV_A_WIDTH, D_A), CONV_A_WIDTH ** -0.5),
        "conv_b_w": normal(ks[4], (DEPTH, CONV_B_WIDTH, D_B), CONV_B_WIDTH ** -0.5),
        "conv_b_bias": normal(ks[5], (DEPTH, D_B), 0.02),
        "ln_b_gain": gain(ks[6], D_B),
        "ln_b_bias": normal(ks[7], (DEPTH, D_B), 0.02),
        "w_out": normal(ks[8], (DEPTH, D_MIX, D_MODEL), D_MIX ** -0.5),
        "mix_post_gain": gain(ks[9], D_MODEL),
        "mlp_pre_gain": gain(ks[10], D_MODEL),
        "w_up": normal(ks[11], (DEPTH, D_MODEL, D_FF), D_MODEL ** -0.5),
        "w_down": normal(ks[12], (DEPTH, D_FF, D_MODEL), D_FF ** -0.5),
        "mlp_post_gain": gain(ks[13], D_MODEL),
    }


def reference(x, mix_pre_gain, w_in, conv_a_w, conv_b_w, conv_b_bias, ln_b_gain,
              ln_b_bias, w_out, mix_post_gain, mlp_pre_gain, w_up, w_down,
              mlp_post_gain):
    h = x
    for l in range(DEPTH):
        u = rms_norm(h, mix_pre_gain[l])
        p = jnp.einsum("bsd,de->bse", u, w_in[l])
        b_gate, c_gate, v_a, val_b, gate_b = jnp.split(
            p, [D_A, 2 * D_A, 3 * D_A, 3 * D_A + D_B], axis=-1)

        y_a = b_gate * causal_dwconv(c_gate * v_a, conv_a_w[l])

        g = val_b * jax.nn.sigmoid(gate_b)
        g = causal_dwconv(g, conv_b_w[l]) + conv_b_bias[l].astype(g.dtype)
        y_b = jax.nn.silu(layer_norm(g, ln_b_gain[l], ln_b_bias[l]))

        mixed = jnp.concatenate([y_a, y_b], axis=-1)
        o = jnp.einsum("bse,ed->bsd", mixed, w_out[l])
        h = h + rms_norm(o, mix_post_gain[l])

        m = rms_norm(h, mlp_pre_gain[l])
        z = jnp.square(jax.nn.relu(jnp.einsum("bsd,df->bsf", m, w_up[l])))
        z = jnp.einsum("bsf,fd->bsd", z, w_down[l])
        h = h + rms_norm(z, mlp_post_gain[l])
    return h
```

```python
import functools

import jax
import jax.numpy as jnp
from jax import lax
from jax.experimental import pallas as pl
from jax.experimental.pallas import tpu as pltpu

EPS = 1e-6
CONV_A_WIDTH = 3
CONV_B_WIDTH = 31
N_IN_PIECES = 5

SUBLANES = 8
MXU_WIDTH = 256

MIX_ROWS = 256
MIX_COLS = MXU_WIDTH
HALO_A = SUBLANES
HALO_B = 4 * SUBLANES
MLP_ROWS = 512
MLP_FF = 1024
VMEM_LIMIT_BYTES = 56 * 1024 * 1024


def _rms_scale(v):
    return lax.rsqrt(jnp.mean(v * v, axis=-1, keepdims=True) + EPS)


def _mixer_kernel(x_ref, g_pre_ref, w_in_ref, caw_ref, cbw_ref, cbb_ref, lng_ref,
                  lnb_ref, w_out_ref, g_post_ref, o_ref,
                  exta_s, extb_s, gc_s, mixed_s, *, d_a, d_b):
    rows = x_ref.shape[0]
    n_chunks = d_a // MIX_COLS

    @pl.when(pl.program_id(1) == 0)
    def _():
        exta_s[0:HALO_A, :] = jnp.zeros((HALO_A, d_a), jnp.float32)
        extb_s[0:HALO_B, :] = jnp.zeros((HALO_B, d_b), jnp.float32)

    x = x_ref[...]
    u = (x * _rms_scale(x) * g_pre_ref[...]).astype(jnp.bfloat16)

    for j in range(n_chunks):
        cs = slice(j * MIX_COLS, (j + 1) * MIX_COLS)
        w0 = j * N_IN_PIECES * MIX_COLS
        p = jnp.dot(u, w_in_ref[:, w0:w0 + N_IN_PIECES * MIX_COLS],
                    preferred_element_type=jnp.float32)
        b_gate, c_gate, v_a, val_b, gate_b = (
            p[:, q * MIX_COLS:(q + 1) * MIX_COLS] for q in range(N_IN_PIECES))

        exta_s[HALO_A:HALO_A + rows, cs] = c_gate * v_a
        conv_a = jnp.zeros((rows, MIX_COLS), jnp.float32)
        for k in range(CONV_A_WIDTH):
            r0 = HALO_A - (CONV_A_WIDTH - 1) + k
            conv_a = conv_a + caw_ref[k:k + 1, cs] * exta_s[r0:r0 + rows, cs]
        mixed_s[:, cs] = (b_gate * conv_a).astype(jnp.bfloat16)
        exta_s[0:HALO_A, cs] = exta_s[rows:rows + HALO_A, cs]

        extb_s[HALO_B:HALO_B + rows, cs] = val_b * jax.nn.sigmoid(gate_b)
        conv_b = jnp.broadcast_to(cbb_ref[:, cs], (rows, MIX_COLS))
        for k in range(CONV_B_WIDTH):
            r0 = HALO_B - (CONV_B_WIDTH - 1) + k
            conv_b = conv_b + cbw_ref[k:k + 1, cs] * extb_s[r0:r0 + rows, cs]
        gc_s[:, cs] = conv_b
        extb_s[0:HALO_B, cs] = extb_s[rows:rows + HALO_B, cs]

    g = gc_s[...]
    mu = jnp.mean(g, axis=-1, keepdims=True)
    gd = g - mu
    var = jnp.mean(gd * gd, axis=-1, keepdims=True)
    y = gd * lax.rsqrt(var + EPS) * lng_ref[...] + lnb_ref[...]
    mixed_s[:, d_a:d_a + d_b] = (y * jax.nn.sigmoid(y)).astype(jnp.bfloat16)

    o = jnp.dot(mixed_s[...], w_out_ref[...], preferred_element_type=jnp.float32)
    o_ref[...] = x + o * _rms_scale(o) * g_post_ref[...]


def _mlp_kernel(h_ref, g_pre_ref, w_up_ref, w_down_ref, g_post_ref, o_ref, m_s, acc_s):
    j = pl.program_id(1)

    @pl.when(j == 0)
    def _():
        h = h_ref[...]
        m_s[...] = (h * _rms_scale(h) * g_pre_ref[...]).astype(jnp.bfloat16)
        acc_s[...] = jnp.zeros(acc_s.shape, jnp.float32)

    z = jnp.dot(m_s[...], w_up_ref[...], preferred_element_type=jnp.float32)
    z = jnp.square(jnp.maximum(z, 0.0)).astype(jnp.bfloat16)
    acc_s[...] += jnp.dot(z, w_down_ref[...], preferred_element_type=jnp.float32)

    @pl.when(j == pl.num_programs(1) - 1)
    def _():
        zz = acc_s[...]
        o_ref[...] = h_ref[...] + zz * _rms_scale(zz) * g_post_ref[...]


def _resident(shape):
    return pl.BlockSpec(shape, lambda *_: (0,) * len(shape), pipeline_mode=pl.Buffered(1))


def _mixer(h2d, batch, seq, g_pre, w_in_p, caw, cbw, cbb, lng, lnb, w_out, g_post):
    n_rows, d_model = h2d.shape
    d_a = caw.shape[1]
    d_b = cbw.shape[1]
    tiles_per_seq = seq // MIX_ROWS
    row_spec = pl.BlockSpec((MIX_ROWS, d_model), lambda b, t: (b * tiles_per_seq + t, 0))
    return pl.pallas_call(
        functools.partial(_mixer_kernel, d_a=d_a, d_b=d_b),
        out_shape=jax.ShapeDtypeStruct((n_rows, d_model), jnp.float32),
        grid=(batch, tiles_per_seq),
        in_specs=[row_spec, _resident(g_pre.shape), _resident(w_in_p.shape),
                  _resident(caw.shape), _resident(cbw.shape), _resident(cbb.shape),
                  _resident(lng.shape), _resident(lnb.shape), _resident(w_out.shape),
                  _resident(g_post.shape)],
        out_specs=row_spec,
        scratch_shapes=[
            pltpu.VMEM((HALO_A + MIX_ROWS, d_a), jnp.float32),
            pltpu.VMEM((HALO_B + MIX_ROWS, d_b), jnp.float32),
            pltpu.VMEM((MIX_ROWS, d_b), jnp.float32),
            pltpu.VMEM((MIX_ROWS, d_a + d_b), jnp.bfloat16),
        ],
        compiler_params=pltpu.CompilerParams(
            dimension_semantics=("arbitrary", "arbitrary"),
            vmem_limit_bytes=VMEM_LIMIT_BYTES),
        name="mixer",
    )(h2d, g_pre, w_in_p, caw, cbw, cbb, lng, lnb, w_out, g_post)


def _mlp(h2d, g_pre, w_up, w_down, g_post):
    n_rows, d_model = h2d.shape
    d_ff = w_up.shape[1]
    row_spec = pl.BlockSpec((MLP_ROWS, d_model), lambda i, j: (i, 0))
    return pl.pallas_call(
        _mlp_kernel,
        out_shape=jax.ShapeDtypeStruct((n_rows, d_model), jnp.float32),
        grid=(n_rows // MLP_ROWS, d_ff // MLP_FF),
        in_specs=[row_spec, _resident(g_pre.shape),
                  pl.BlockSpec((d_model, MLP_FF), lambda i, j: (0, j)),
                  pl.BlockSpec((MLP_FF, d_model), lambda i, j: (j, 0)),
                  _resident(g_post.shape)],
        out_specs=row_spec,
        scratch_shapes=[
            pltpu.VMEM((MLP_ROWS, d_model), jnp.bfloat16),
            pltpu.VMEM((MLP_ROWS, d_model), jnp.float32),
        ],
        compiler_params=pltpu.CompilerParams(
            dimension_semantics=("arbitrary", "arbitrary"),
            vmem_limit_bytes=VMEM_LIMIT_BYTES),
        name="mlp",
    )(h2d, g_pre, w_up, w_down, g_post)


def _chunk_major(w_in, d_a):
    d_model = w_in.shape[0]
    n_chunks = d_a // MIX_COLS
    w = w_in.reshape(d_model, N_IN_PIECES, n_chunks, MIX_COLS)
    return w.transpose(0, 2, 1, 3).reshape(d_model, N_IN_PIECES * d_a)


def kernel(x, mix_pre_gain, w_in, conv_a_w, conv_b_w, conv_b_bias, ln_b_gain, ln_b_bias,
           w_out, mix_post_gain, mlp_pre_gain, w_up, w_down, mlp_post_gain):
    batch, seq, d_model = x.shape
    depth = w_in.shape[0]
    d_a = conv_a_w.shape[2]
    d_b = conv_b_w.shape[2]
    assert d_a == d_b and w_in.shape[2] == N_IN_PIECES * d_a
    assert d_a % MIX_COLS == 0 and seq % MIX_ROWS == 0 and MIX_ROWS >= HALO_B
    assert (batch * seq) % MLP_ROWS == 0 and w_up.shape[2] % MLP_FF == 0
    bf16 = jnp.bfloat16
    row = lambda a: a.reshape(1, -1)

    h = x.reshape(batch * seq, d_model)
    for l in range(depth):
        h = _mixer(h, batch, seq, row(mix_pre_gain[l]),
                   _chunk_major(w_in[l], d_a).astype(bf16),
                   conv_a_w[l], conv_b_w[l], row(conv_b_bias[l]),
                   row(ln_b_gain[l]), row(ln_b_bias[l]),
                   w_out[l].astype(bf16), row(mix_post_gain[l]))
        h = _mlp(h, row(mlp_pre_gain[l]), w_up[l].astype(bf16), w_down[l].astype(bf16),
                 row(mlp_post_gain[l]))
    return h.reshape(batch, seq, d_model)
```

```python
import functools

import jax
import jax.numpy as jnp
from jax import lax
from jax.experimental import pallas as pl
from jax.experimental.pallas import tpu as pltpu

EPS = 1e-6
N_IN_PIECES = 5

SUBLANES = 8
LANES = 128
MXU_WIDTH = 256

MIX_ROWS = 256
MIX_COLS = MXU_WIDTH
CONV_ROWS = 128
NORM_ROWS = 32
HALO_A = SUBLANES
HALO_B = 4 * SUBLANES
MLP_ROWS = 512
MLP_FF = 1024
VMEM_LIMIT_BYTES = 56 * 1024 * 1024


def _rms_scale(v):
    return lax.rsqrt(jnp.mean(v * v, axis=-1, keepdims=True) + EPS)


def _causal_conv(ext_ref, w_ref, cols, rows, halo, init_ref, emit):
    width = w_ref.shape[0]
    first = halo - (width - 1)
    part_rows = CONV_ROWS + SUBLANES
    for lane0 in range(cols.start, cols.stop, LANES):
        lanes = slice(lane0, lane0 + LANES)
        for row0 in range(0, rows, CONV_ROWS):
            if init_ref is None:
                acc = jnp.zeros((CONV_ROWS, LANES), jnp.float32)
            else:
                acc = jnp.broadcast_to(init_ref[:, lanes], (CONV_ROWS, LANES))
            for r in range(SUBLANES):
                taps = [s for s in range(r, halo + 1, SUBLANES) if 0 <= s - first < width]
                if not taps:
                    continue
                n = CONV_ROWS if r == 0 else part_rows
                part = None
                for s in taps:
                    a0 = row0 + s - r
                    term = w_ref[s - first:s - first + 1, lanes] * ext_ref[a0:a0 + n, lanes]
                    part = term if part is None else part + term
                if r == 0:
                    acc = acc + part
                else:
                    acc = acc + pltpu.roll(part, part_rows - r, axis=0)[0:CONV_ROWS]
            emit(slice(row0, row0 + CONV_ROWS), lanes, acc)


def _mixer_kernel(x_ref, xp_ref, g_pre_ref, w_in_ref, caw_ref, cbw_ref, cbb_ref, lng_ref,
                  lnb_ref, w_out_ref, g_post_ref, o_ref,
                  u_s, exta_s, gc_s, extb0_s, extb1_s, mixed0_s, mixed1_s,
                  *, d_a, d_b, tiles_per_seq):
    i = pl.program_id(0)
    rows = MIX_ROWS
    n_chunks = d_a // MIX_COLS
    d_model = x_ref.shape[1]

    @pl.when(i == 0)
    def _():
        exta_s[0:HALO_A, :] = jnp.zeros((HALO_A, d_a), jnp.float32)
        extb1_s[...] = jnp.zeros(extb1_s.shape, jnp.float32)
        mixed1_s[...] = jnp.zeros(mixed1_s.shape, jnp.bfloat16)

    def step(cur_extb, cur_mixed, prev_extb, prev_mixed):
        seq_start = (i % tiles_per_seq) == 0
        exta_s[0:HALO_A, :] = jnp.where(seq_start, 0.0, exta_s[0:HALO_A, :])
        cur_extb[0:HALO_B, :] = jnp.where(seq_start, 0.0, prev_extb[rows:rows + HALO_B, :])

        for r0 in range(0, rows, NORM_ROWS):
            xb = x_ref[r0:r0 + NORM_ROWS, :]
            u_s[r0:r0 + NORM_ROWS, :] = (xb * _rms_scale(xb) * g_pre_ref[...]).astype(jnp.bfloat16)

        for j in range(n_chunks):
            cs = slice(j * MIX_COLS, (j + 1) * MIX_COLS)

            u = u_s[...]
            b_gate, c_gate, v_a, val_b, gate_b = (
                jnp.dot(u, w_in_ref[:, q * d_a + cs.start:q * d_a + cs.stop],
                        preferred_element_type=jnp.float32) for q in range(N_IN_PIECES))
            cur_extb[HALO_B:HALO_B + rows, cs] = val_b * jax.nn.sigmoid(gate_b)
            exta_s[HALO_A:HALO_A + rows, cs] = c_gate * v_a

            def emit_a(rs, lanes, acc, b_gate=b_gate, c0=cs.start):
                gate = b_gate[rs, lanes.start - c0:lanes.stop - c0]
                cur_mixed[rs, lanes] = (gate * acc).astype(jnp.bfloat16)

            _causal_conv(exta_s, caw_ref, cs, rows, HALO_A, None, emit_a)
            exta_s[0:HALO_A, cs] = exta_s[rows:rows + HALO_A, cs]

            def emit_b(rs, lanes, acc):
                gc_s[rs, lanes] = acc

            _causal_conv(prev_extb, cbw_ref, cs, rows, HALO_B, cbb_ref, emit_b)

        for r0 in range(0, rows, NORM_ROWS):
            g = gc_s[r0:r0 + NORM_ROWS, :]
            gd = g - jnp.mean(g, axis=-1, keepdims=True)
            var = jnp.mean(gd * gd, axis=-1, keepdims=True)
            y = gd * lax.rsqrt(var + EPS) * lng_ref[...] + lnb_ref[...]
            prev_mixed[r0:r0 + NORM_ROWS, d_a:d_a + d_b] = (
                y * jax.nn.sigmoid(y)).astype(jnp.bfloat16)
        o = jnp.dot(prev_mixed[...], w_out_ref[...], preferred_element_type=jnp.float32)
        o_ref[...] = xp_ref[...] + o * _rms_scale(o) * g_post_ref[...]

    @pl.when(i % 2 == 0)
    def _():
        step(extb0_s, mixed0_s, extb1_s, mixed1_s)

    @pl.when(i % 2 == 1)
    def _():
        step(extb1_s, mixed1_s, extb0_s, mixed0_s)


def _mlp_kernel(h_ref, g_pre_ref, w_up_ref, w_down_ref, g_post_ref, o_ref, m_s, acc_s):
    j = pl.program_id(1)

    @pl.when(j == 0)
    def _():
        h = h_ref[...]
        m_s[...] = (h * _rms_scale(h) * g_pre_ref[...]).astype(jnp.bfloat16)
        acc_s[...] = jnp.zeros(acc_s.shape, jnp.float32)

    z = jnp.dot(m_s[...], w_up_ref[...], preferred_element_type=jnp.float32)
    z = jnp.square(jnp.maximum(z, 0.0)).astype(jnp.bfloat16)
    acc_s[...] += jnp.dot(z, w_down_ref[...], preferred_element_type=jnp.float32)

    @pl.when(j == pl.num_programs(1) - 1)
    def _():
        zz = acc_s[...]
        o_ref[...] = h_ref[...] + zz * _rms_scale(zz) * g_post_ref[...]


def _resident(shape):
    return pl.BlockSpec(shape, lambda *_: (0,) * len(shape), pipeline_mode=pl.Buffered(1))


def _mixer(h2d, seq, g_pre, w_in, caw, cbw, cbb, lng, lnb, w_out, g_post):
    n_rows, d_model = h2d.shape
    d_a = caw.shape[1]
    d_b = cbw.shape[1]
    n_tiles = n_rows // MIX_ROWS
    cur_spec = pl.BlockSpec((MIX_ROWS, d_model), lambda i: (jnp.minimum(i, n_tiles - 1), 0))
    prev_spec = pl.BlockSpec((MIX_ROWS, d_model), lambda i: (jnp.maximum(i - 1, 0), 0))
    return pl.pallas_call(
        functools.partial(_mixer_kernel, d_a=d_a, d_b=d_b, tiles_per_seq=seq // MIX_ROWS),
        out_shape=jax.ShapeDtypeStruct((n_rows, d_model), jnp.float32),
        grid=(n_tiles + 1,),
        in_specs=[cur_spec, prev_spec, _resident(g_pre.shape), _resident(w_in.shape),
                  _resident(caw.shape), _resident(cbw.shape), _resident(cbb.shape),
                  _resident(lng.shape), _resident(lnb.shape), _resident(w_out.shape),
                  _resident(g_post.shape)],
        out_specs=prev_spec,
        scratch_shapes=[
            pltpu.VMEM((MIX_ROWS, d_model), jnp.bfloat16),
            pltpu.VMEM((HALO_A + MIX_ROWS, d_a), jnp.float32),
            pltpu.VMEM((MIX_ROWS, d_b), jnp.float32),
            pltpu.VMEM((HALO_B + MIX_ROWS, d_b), jnp.float32),
            pltpu.VMEM((HALO_B + MIX_ROWS, d_b), jnp.float32),
            pltpu.VMEM((MIX_ROWS, d_a + d_b), jnp.bfloat16),
            pltpu.VMEM((MIX_ROWS, d_a + d_b), jnp.bfloat16),
        ],
        compiler_params=pltpu.CompilerParams(
            dimension_semantics=("arbitrary",),
            vmem_limit_bytes=VMEM_LIMIT_BYTES),
        name="mixer",
    )(h2d, h2d, g_pre, w_in, caw, cbw, cbb, lng, lnb, w_out, g_post)


def _mlp(h2d, g_pre, w_up, w_down, g_post):
    n_rows, d_model = h2d.shape
    d_ff = w_up.shape[1]
    row_spec = pl.BlockSpec((MLP_ROWS, d_model), lambda i, j: (i, 0))
    return pl.pallas_call(
        _mlp_kernel,
        out_shape=jax.ShapeDtypeStruct((n_rows, d_model), jnp.float32),
        grid=(n_rows // MLP_ROWS, d_ff // MLP_FF),
        in_specs=[row_spec, _resident(g_pre.shape),
                  pl.BlockSpec((d_model, MLP_FF), lambda i, j: (0, j)),
                  pl.BlockSpec((MLP_FF, d_model), lambda i, j: (j, 0)),
                  _resident(g_post.shape)],
        out_specs=row_spec,
        scratch_shapes=[
            pltpu.VMEM((MLP_ROWS, d_model), jnp.bfloat16),
            pltpu.VMEM((MLP_ROWS, d_model), jnp.float32),
        ],
        compiler_params=pltpu.CompilerParams(
            dimension_semantics=("arbitrary", "arbitrary"),
            vmem_limit_bytes=VMEM_LIMIT_BYTES),
        name="mlp",
    )(h2d, g_pre, w_up, w_down, g_post)


def kernel(x, mix_pre_gain, w_in, conv_a_w, conv_b_w, conv_b_bias, ln_b_gain, ln_b_bias,
           w_out, mix_post_gain, mlp_pre_gain, w_up, w_down, mlp_post_gain):
    batch, seq, d_model = x.shape
    depth = w_in.shape[0]
    d_a = conv_a_w.shape[2]
    d_b = conv_b_w.shape[2]
    assert d_a == d_b and w_in.shape[2] == N_IN_PIECES * d_a
    assert conv_a_w.shape[1] - 1 <= HALO_A and conv_b_w.shape[1] - 1 <= HALO_B
    assert d_a % MIX_COLS == 0 and seq % MIX_ROWS == 0 and MIX_ROWS >= HALO_B
    assert MIX_ROWS % CONV_ROWS == 0 and MIX_ROWS % NORM_ROWS == 0
    assert (batch * seq) % MLP_ROWS == 0 and w_up.shape[2] % MLP_FF == 0
    bf16 = jnp.bfloat16
    row = lambda a: a.reshape(1, -1)

    h = x.reshape(batch * seq, d_model)
    for l in range(depth):
        h = _mixer(h, seq, row(mix_pre_gain[l]), w_in[l].astype(bf16),
                   conv_a_w[l], conv_b_w[l], row(conv_b_bias[l]),
                   row(ln_b_gain[l]), row(ln_b_bias[l]),
                   w_out[l].astype(bf16), row(mix_post_gain[l]))
        h = _mlp(h, row(mlp_pre_gain[l]), w_up[l].astype(bf16), w_down[l].astype(bf16),
                 row(mlp_post_gain[l]))
    return h.reshape(batch, seq, d_model)
```

```python
import functools

import jax
import jax.numpy as jnp
from jax import lax
from jax.experimental import pallas as pl
from jax.experimental.pallas import tpu as pltpu

EPS = 1e-6
N_IN_PIECES = 5

SUBLANES = 8
LANES = 128
MXU_WIDTH = 256

MIX_ROWS = 256
MIX_COLS = MXU_WIDTH
CONV_ROWS = 128
NORM_ROWS = 32
HALO_A = SUBLANES
HALO_B = 4 * SUBLANES
MLP_ROWS = 512
MLP_FF = 1024
VMEM_LIMIT_BYTES = 56 * 1024 * 1024


def _rms_scale(v):
    return lax.rsqrt(jnp.mean(v * v, axis=-1, keepdims=True) + EPS)


def _bits_fold(v):
    bits = pltpu.bitcast(v, jnp.uint32)
    fold = bits[0:SUBLANES]
    for r0 in range(SUBLANES, v.shape[0], SUBLANES):
        fold = fold | bits[r0:r0 + SUBLANES]
    return fold


def _after(src_ref, dst_ref, token):
    zero = ((token >> 16) >> 16).astype(jnp.float32)
    reps = (src_ref.shape[0] // SUBLANES, src_ref.shape[1] // LANES)
    dst_ref[...] = src_ref[...] + jnp.tile(zero, reps).astype(src_ref.dtype)


def _causal_conv(ext_ref, w_ref, cols, rows, halo, init_ref, emit):
    width = w_ref.shape[0]
    first = halo - (width - 1)
    part_rows = CONV_ROWS + SUBLANES
    token = None
    for lane0 in range(cols.start, cols.stop, LANES):
        lanes = slice(lane0, lane0 + LANES)
        for row0 in range(0, rows, CONV_ROWS):
            if init_ref is None:
                acc = jnp.zeros((CONV_ROWS, LANES), jnp.float32)
            else:
                acc = jnp.broadcast_to(init_ref[:, lanes], (CONV_ROWS, LANES))
            for r in range(SUBLANES):
                taps = [s for s in range(r, halo + 1, SUBLANES) if 0 <= s - first < width]
                if not taps:
                    continue
                n = CONV_ROWS if r == 0 else part_rows
                part = None
                for s in taps:
                    a0 = row0 + s - r
                    term = w_ref[s - first:s - first + 1, lanes] * ext_ref[a0:a0 + n, lanes]
                    part = term if part is None else part + term
                if r == 0:
                    acc = acc + part
                else:
                    acc = acc + pltpu.roll(part, part_rows - r, axis=0)[0:CONV_ROWS]
            emit(slice(row0, row0 + CONV_ROWS), lanes, acc)
            fold = _bits_fold(acc)
            token = fold if token is None else token | fold
    return token


def _mixer_kernel(x_ref, xp_ref, g_pre_ref, w_in_ref, caw_ref, cbw_ref, cbb_ref, lng_ref,
                  lnb_ref, w_out_ref, g_post_ref, o_ref,
                  u0_s, u1_s, exta_s, extb_s, gc_s, oa_s, mixed_s,
                  *, d_a, d_b, tiles_per_seq):
    i = pl.program_id(0)
    rows = MIX_ROWS
    n_chunks = d_a // MIX_COLS

    @pl.when(i == 0)
    def _():
        exta_s[0:HALO_A, :] = jnp.zeros((HALO_A, d_a), jnp.float32)
        extb_s[...] = jnp.zeros(extb_s.shape, jnp.float32)
        mixed_s[...] = jnp.zeros(mixed_s.shape, jnp.bfloat16)

    seq_start = (i % tiles_per_seq) == 0
    exta_s[0:HALO_A, :] = jnp.where(seq_start, 0.0, exta_s[0:HALO_A, :])

    for r0 in range(0, rows, NORM_ROWS):
        xb = x_ref[r0:r0 + NORM_ROWS, :]
        u1_s[r0:r0 + NORM_ROWS, :] = (xb * _rms_scale(xb) * g_pre_ref[...]).astype(jnp.bfloat16)

    oa_s[...] = jnp.dot(mixed_s[:, 0:d_a], w_out_ref[0:d_a, :],
                        preferred_element_type=jnp.float32)

    u_bufs = (u1_s, u0_s)
    for j in range(n_chunks):
        cs = slice(j * MIX_COLS, (j + 1) * MIX_COLS)

        def emit_b(rs, lanes, acc):
            gc_s[rs, lanes] = acc

        conv_done = _causal_conv(extb_s, cbw_ref, cs, rows, HALO_B, cbb_ref, emit_b)

        u_src, u_dst = u_bufs[j % 2], u_bufs[(j + 1) % 2]
        _after(u_src, u_dst, conv_done)
        b_gate, c_gate, v_a, val_b, gate_b = (
            jnp.dot(u_dst[...], w_in_ref[:, q * d_a + cs.start:q * d_a + cs.stop],
                    preferred_element_type=jnp.float32) for q in range(N_IN_PIECES))
        extb_s[0:HALO_B, cs] = jnp.where(seq_start, 0.0, extb_s[rows:rows + HALO_B, cs])
        extb_s[HALO_B:HALO_B + rows, cs] = val_b * jax.nn.sigmoid(gate_b)
        exta_s[HALO_A:HALO_A + rows, cs] = c_gate * v_a

        def emit_a(rs, lanes, acc, b_gate=b_gate, c0=cs.start):
            gate = b_gate[rs, lanes.start - c0:lanes.stop - c0]
            mixed_s[rs, lanes] = (gate * acc).astype(jnp.bfloat16)

        _causal_conv(exta_s, caw_ref, cs, rows, HALO_A, None, emit_a)
        exta_s[0:HALO_A, cs] = exta_s[rows:rows + HALO_A, cs]

    for r0 in range(0, rows, NORM_ROWS):
        g = gc_s[r0:r0 + NORM_ROWS, :]
        gd = g - jnp.mean(g, axis=-1, keepdims=True)
        var = jnp.mean(gd * gd, axis=-1, keepdims=True)
        y = gd * lax.rsqrt(var + EPS) * lng_ref[...] + lnb_ref[...]
        mixed_s[r0:r0 + NORM_ROWS, d_a:d_a + d_b] = (
            y * jax.nn.sigmoid(y)).astype(jnp.bfloat16)
    o = oa_s[...] + jnp.dot(mixed_s[:, d_a:d_a + d_b], w_out_ref[d_a:d_a + d_b, :],
                            preferred_element_type=jnp.float32)
    o_ref[...] = xp_ref[...] + o * _rms_scale(o) * g_post_ref[...]


def _mlp_kernel(h_ref, g_pre_ref, w_up_ref, w_down_ref, g_post_ref, o_ref, m_s, acc_s):
    j = pl.program_id(1)

    @pl.when(j == 0)
    def _():
        h = h_ref[...]
        m_s[...] = (h * _rms_scale(h) * g_pre_ref[...]).astype(jnp.bfloat16)
        acc_s[...] = jnp.zeros(acc_s.shape, jnp.float32)

    z = jnp.dot(m_s[...], w_up_ref[...], preferred_element_type=jnp.float32)
    z = jnp.square(jnp.maximum(z, 0.0)).astype(jnp.bfloat16)
    acc_s[...] += jnp.dot(z, w_down_ref[...], preferred_element_type=jnp.float32)

    @pl.when(j == pl.num_programs(1) - 1)
    def _():
        zz = acc_s[...]
        o_ref[...] = h_ref[...] + zz * _rms_scale(zz) * g_post_ref[...]


def _resident(shape):
    return pl.BlockSpec(shape, lambda *_: (0,) * len(shape), pipeline_mode=pl.Buffered(1))


def _mixer(h2d, seq, g_pre, w_in, caw, cbw, cbb, lng, lnb, w_out, g_post):
    n_rows, d_model = h2d.shape
    d_a = caw.shape[1]
    d_b = cbw.shape[1]
    n_tiles = n_rows // MIX_ROWS
    cur_spec = pl.BlockSpec((MIX_ROWS, d_model), lambda i: (jnp.minimum(i, n_tiles - 1), 0))
    prev_spec = pl.BlockSpec((MIX_ROWS, d_model), lambda i: (jnp.maximum(i - 1, 0), 0))
    return pl.pallas_call(
        functools.partial(_mixer_kernel, d_a=d_a, d_b=d_b, tiles_per_seq=seq // MIX_ROWS),
        out_shape=jax.ShapeDtypeStruct((n_rows, d_model), jnp.float32),
        grid=(n_tiles + 1,),
        in_specs=[cur_spec, prev_spec, _resident(g_pre.shape), _resident(w_in.shape),
                  _resident(caw.shape), _resident(cbw.shape), _resident(cbb.shape),
                  _resident(lng.shape), _resident(lnb.shape), _resident(w_out.shape),
                  _resident(g_post.shape)],
        out_specs=prev_spec,
        scratch_shapes=[
            pltpu.VMEM((MIX_ROWS, d_model), jnp.bfloat16),
            pltpu.VMEM((MIX_ROWS, d_model), jnp.bfloat16),
            pltpu.VMEM((HALO_A + MIX_ROWS, d_a), jnp.float32),
            pltpu.VMEM((HALO_B + MIX_ROWS, d_b), jnp.float32),
            pltpu.VMEM((MIX_ROWS, d_b), jnp.float32),
            pltpu.VMEM((MIX_ROWS, d_model), jnp.float32),
            pltpu.VMEM((MIX_ROWS, d_a + d_b), jnp.bfloat16),
        ],
        compiler_params=pltpu.CompilerParams(
            dimension_semantics=("arbitrary",),
            vmem_limit_bytes=VMEM_LIMIT_BYTES),
        name="mixer",
    )(h2d, h2d, g_pre, w_in, caw, cbw, cbb, lng, lnb, w_out, g_post)


def _mlp(h2d, g_pre, w_up, w_down, g_post):
    n_rows, d_model = h2d.shape
    d_ff = w_up.shape[1]
    row_spec = pl.BlockSpec((MLP_ROWS, d_model), lambda i, j: (i, 0))
    return pl.pallas_call(
        _mlp_kernel,
        out_shape=jax.ShapeDtypeStruct((n_rows, d_model), jnp.float32),
        grid=(n_rows // MLP_ROWS, d_ff // MLP_FF),
        in_specs=[row_spec, _resident(g_pre.shape),
                  pl.BlockSpec((d_model, MLP_FF), lambda i, j: (0, j)),
                  pl.BlockSpec((MLP_FF, d_model), lambda i, j: (j, 0)),
                  _resident(g_post.shape)],
        out_specs=row_spec,
        scratch_shapes=[
            pltpu.VMEM((MLP_ROWS, d_model), jnp.bfloat16),
            pltpu.VMEM((MLP_ROWS, d_model), jnp.float32),
        ],
        compiler_params=pltpu.CompilerParams(
            dimension_semantics=("arbitrary", "arbitrary"),
            vmem_limit_bytes=VMEM_LIMIT_BYTES),
        name="mlp",
    )(h2d, g_pre, w_up, w_down, g_post)


def kernel(x, mix_pre_gain, w_in, conv_a_w, conv_b_w, conv_b_bias, ln_b_gain, ln_b_bias,
           w_out, mix_post_gain, mlp_pre_gain, w_up, w_down, mlp_post_gain):
    batch, seq, d_model = x.shape
    depth = w_in.shape[0]
    d_a = conv_a_w.shape[2]
    d_b = conv_b_w.shape[2]
    assert d_a == d_b and w_in.shape[2] == N_IN_PIECES * d_a
    assert conv_a_w.shape[1] - 1 <= HALO_A and conv_b_w.shape[1] - 1 <= HALO_B
    assert d_a % MIX_COLS == 0 and seq % MIX_ROWS == 0 and MIX_ROWS >= HALO_B
    assert MIX_ROWS % CONV_ROWS == 0 and MIX_ROWS % NORM_ROWS == 0
    assert (batch * seq) % MLP_ROWS == 0 and w_up.shape[2] % MLP_FF == 0
    bf16 = jnp.bfloat16
    row = lambda a: a.reshape(1, -1)

    h = x.reshape(batch * seq, d_model)
    for l in range(depth):
        h = _mixer(h, seq, row(mix_pre_gain[l]), w_in[l].astype(bf16),
                   conv_a_w[l], conv_b_w[l], row(conv_b_bias[l]),
                   row(ln_b_gain[l]), row(ln_b_bias[l]),
                   w_out[l].astype(bf16), row(mix_post_gain[l]))
        h = _mlp(h, row(mlp_pre_gain[l]), w_up[l].astype(bf16), w_down[l].astype(bf16),
                 row(mlp_post_gain[l]))
    return h.reshape(batch, seq, d_model)
```

```python
import functools

import jax
import jax.numpy as jnp
from jax import lax
from jax.experimental import pallas as pl
from jax.experimental.pallas import tpu as pltpu

EPS = 1e-6
N_IN_PIECES = 5

SUBLANES = 8
LANES = 128
MXU_WIDTH = 256

MIX_ROWS = 256
MIX_COLS = MXU_WIDTH
CONV_ROWS = 128
NORM_ROWS = 32
HALO_A = SUBLANES
HALO_B = 4 * SUBLANES
MLP_ROWS = 512
MLP_FF = 1024
VMEM_LIMIT_BYTES = 56 * 1024 * 1024


def _rms_scale(v):
    return lax.rsqrt(jnp.mean(v * v, axis=-1, keepdims=True) + EPS)


def _bits_fold(v):
    bits = pltpu.bitcast(v, jnp.uint32)
    fold = bits[0:SUBLANES]
    for r0 in range(SUBLANES, v.shape[0], SUBLANES):
        fold = fold | bits[r0:r0 + SUBLANES]
    return fold


def _after(value, token):
    zero = ((token >> 16) >> 16).astype(jnp.float32)
    reps = (value.shape[0] // SUBLANES, value.shape[1] // LANES)
    return value + jnp.tile(zero, reps).astype(value.dtype)


def _causal_conv(ext_ref, w_ref, cols, rows, halo, init_ref, emit, on_first=None):
    width = w_ref.shape[0]
    first = halo - (width - 1)
    part_rows = CONV_ROWS + SUBLANES
    token = None
    for lane0 in range(cols.start, cols.stop, LANES):
        lanes = slice(lane0, lane0 + LANES)
        for row0 in range(0, rows, CONV_ROWS):
            if init_ref is None:
                acc = jnp.zeros((CONV_ROWS, LANES), jnp.float32)
            else:
                acc = jnp.broadcast_to(init_ref[:, lanes], (CONV_ROWS, LANES))
            for r in range(SUBLANES):
                taps = [s for s in range(r, halo + 1, SUBLANES) if 0 <= s - first < width]
                if not taps:
                    continue
                n = CONV_ROWS if r == 0 else part_rows
                part = None
                for s in taps:
                    a0 = row0 + s - r
                    term = w_ref[s - first:s - first + 1, lanes] * ext_ref[a0:a0 + n, lanes]
                    part = term if part is None else part + term
                if r == 0:
                    acc = acc + part
                else:
                    acc = acc + pltpu.roll(part, part_rows - r, axis=0)[0:CONV_ROWS]
            emit(slice(row0, row0 + CONV_ROWS), lanes, acc)
            fold = _bits_fold(acc)
            if token is None and on_first is not None:
                on_first(fold)
            token = fold if token is None else token | fold
    return token


def _mixer_kernel(x_ref, xp_ref, g_pre_ref, w_in_ref, caw_ref, cbw_ref, cbb_ref, lng_ref,
                  lnb_ref, w_out_ref, g_post_ref, w_up_ref, w_down_ref,
                  o_ref, w_up_bf16_ref, w_down_bf16_ref,
                  u0_s, u1_s, exta_s, extb_s, gc_s, oa_s, mixed_s,
                  *, d_a, d_b, tiles_per_seq):
    i = pl.program_id(0)
    rows = MIX_ROWS
    n_chunks = d_a // MIX_COLS

    @pl.when(i == 0)
    def _():
        exta_s[0:HALO_A, :] = jnp.zeros((HALO_A, d_a), jnp.float32)
        extb_s[...] = jnp.zeros(extb_s.shape, jnp.float32)
        mixed_s[...] = jnp.zeros(mixed_s.shape, jnp.bfloat16)

    seq_start = (i % tiles_per_seq) == 0
    exta_s[0:HALO_A, :] = jnp.where(seq_start, 0.0, exta_s[0:HALO_A, :])

    w_up_bf16_ref[...] = w_up_ref[...].astype(jnp.bfloat16)
    w_down_bf16_ref[...] = w_down_ref[...].astype(jnp.bfloat16)

    for r0 in range(0, rows, NORM_ROWS):
        xb = x_ref[r0:r0 + NORM_ROWS, :]
        u1_s[r0:r0 + NORM_ROWS, :] = (xb * _rms_scale(xb) * g_pre_ref[...]).astype(jnp.bfloat16)

    def start_out_proj(first_conv_block):
        oa_s[...] = jnp.dot(_after(mixed_s[:, 0:d_a], first_conv_block), w_out_ref[0:d_a, :],
                            preferred_element_type=jnp.float32)

    u_bufs = (u1_s, u0_s)
    for j in range(n_chunks):
        cs = slice(j * MIX_COLS, (j + 1) * MIX_COLS)

        def emit_b(rs, lanes, acc):
            gc_s[rs, lanes] = acc

        conv_done = _causal_conv(extb_s, cbw_ref, cs, rows, HALO_B, cbb_ref, emit_b,
                                 on_first=start_out_proj if j == 0 else None)

        u_src, u_dst = u_bufs[j % 2], u_bufs[(j + 1) % 2]
        u_dst[...] = _after(u_src[...], conv_done)
        b_gate, c_gate, v_a, val_b, gate_b = (
            jnp.dot(u_dst[...], w_in_ref[:, q * d_a + cs.start:q * d_a + cs.stop],
                    preferred_element_type=jnp.float32) for q in range(N_IN_PIECES))
        extb_s[0:HALO_B, cs] = jnp.where(seq_start, 0.0, extb_s[rows:rows + HALO_B, cs])
        extb_s[HALO_B:HALO_B + rows, cs] = val_b * jax.nn.sigmoid(gate_b)
        exta_s[HALO_A:HALO_A + rows, cs] = c_gate * v_a

        def emit_a(rs, lanes, acc, b_gate=b_gate, c0=cs.start):
            gate = b_gate[rs, lanes.start - c0:lanes.stop - c0]
            mixed_s[rs, lanes] = (gate * acc).astype(jnp.bfloat16)

        _causal_conv(exta_s, caw_ref, cs, rows, HALO_A, None, emit_a)
        exta_s[0:HALO_A, cs] = exta_s[rows:rows + HALO_A, cs]

    for r0 in range(0, rows, NORM_ROWS):
        g = gc_s[r0:r0 + NORM_ROWS, :]
        gd = g - jnp.mean(g, axis=-1, keepdims=True)
        var = jnp.mean(gd * gd, axis=-1, keepdims=True)
        y = gd * lax.rsqrt(var + EPS) * lng_ref[...] + lnb_ref[...]
        mixed_s[r0:r0 + NORM_ROWS, d_a:d_a + d_b] = (
            y * jax.nn.sigmoid(y)).astype(jnp.bfloat16)
    o = oa_s[...] + jnp.dot(mixed_s[:, d_a:d_a + d_b], w_out_ref[d_a:d_a + d_b, :],
                            preferred_element_type=jnp.float32)
    o_ref[...] = xp_ref[...] + o * _rms_scale(o) * g_post_ref[...]


def _mlp_kernel(h_ref, g_pre_ref, w_up_ref, w_down_ref, g_post_ref, o_ref, m_s, acc_s):
    j = pl.program_id(1)

    @pl.when(j == 0)
    def _():
        h = h_ref[...]
        m_s[...] = (h * _rms_scale(h) * g_pre_ref[...]).astype(jnp.bfloat16)
        acc_s[...] = jnp.zeros(acc_s.shape, jnp.float32)

    z = jnp.dot(m_s[...], w_up_ref[...], preferred_element_type=jnp.float32)
    z = jnp.square(jnp.maximum(z, 0.0)).astype(jnp.bfloat16)
    acc_s[...] += jnp.dot(z, w_down_ref[...], preferred_element_type=jnp.float32)

    @pl.when(j == pl.num_programs(1) - 1)
    def _():
        zz = acc_s[...]
        o_ref[...] = h_ref[...] + zz * _rms_scale(zz) * g_post_ref[...]


def _resident(shape):
    return pl.BlockSpec(shape, lambda *_: (0,) * len(shape), pipeline_mode=pl.Buffered(1))


def _mixer(h2d, seq, g_pre, w_in, caw, cbw, cbb, lng, lnb, w_out, g_post, w_up, w_down):
    n_rows, d_model = h2d.shape
    d_a = caw.shape[1]
    d_b = cbw.shape[1]
    d_ff = w_up.shape[1]
    n_tiles = n_rows // MIX_ROWS
    last = lambda i: jnp.minimum(i, n_tiles - 1)
    cur_spec = pl.BlockSpec((MIX_ROWS, d_model), lambda i: (last(i), 0))
    prev_spec = pl.BlockSpec((MIX_ROWS, d_model), lambda i: (jnp.maximum(i - 1, 0), 0))
    w_up_spec = pl.BlockSpec((d_model // n_tiles, d_ff), lambda i: (last(i), 0))
    w_down_spec = pl.BlockSpec((d_ff // n_tiles, d_model), lambda i: (last(i), 0))
    return pl.pallas_call(
        functools.partial(_mixer_kernel, d_a=d_a, d_b=d_b, tiles_per_seq=seq // MIX_ROWS),
        out_shape=(jax.ShapeDtypeStruct((n_rows, d_model), jnp.float32),
                   jax.ShapeDtypeStruct(w_up.shape, jnp.bfloat16),
                   jax.ShapeDtypeStruct(w_down.shape, jnp.bfloat16)),
        grid=(n_tiles + 1,),
        in_specs=[cur_spec, prev_spec, _resident(g_pre.shape), _resident(w_in.shape),
                  _resident(caw.shape), _resident(cbw.shape), _resident(cbb.shape),
                  _resident(lng.shape), _resident(lnb.shape), _resident(w_out.shape),
                  _resident(g_post.shape), w_up_spec, w_down_spec],
        out_specs=(prev_spec, w_up_spec, w_down_spec),
        scratch_shapes=[
            pltpu.VMEM((MIX_ROWS, d_model), jnp.bfloat16),
            pltpu.VMEM((MIX_ROWS, d_model), jnp.bfloat16),
            pltpu.VMEM((HALO_A + MIX_ROWS, d_a), jnp.float32),
            pltpu.VMEM((HALO_B + MIX_ROWS, d_b), jnp.float32),
            pltpu.VMEM((MIX_ROWS, d_b), jnp.float32),
            pltpu.VMEM((MIX_ROWS, d_model), jnp.float32),
            pltpu.VMEM((MIX_ROWS, d_a + d_b), jnp.bfloat16),
        ],
        compiler_params=pltpu.CompilerParams(
            dimension_semantics=("arbitrary",),
            vmem_limit_bytes=VMEM_LIMIT_BYTES),
        name="mixer",
    )(h2d, h2d, g_pre, w_in, caw, cbw, cbb, lng, lnb, w_out, g_post, w_up, w_down)


def _mlp(h2d, g_pre, w_up, w_down, g_post):
    n_rows, d_model = h2d.shape
    d_ff = w_up.shape[1]
    row_spec = pl.BlockSpec((MLP_ROWS, d_model), lambda i, j: (i, 0))
    return pl.pallas_call(
        _mlp_kernel,
        out_shape=jax.ShapeDtypeStruct((n_rows, d_model), jnp.float32),
        grid=(n_rows // MLP_ROWS, d_ff // MLP_FF),
        in_specs=[row_spec, _resident(g_pre.shape),
                  pl.BlockSpec((d_model, MLP_FF), lambda i, j: (0, j)),
                  pl.BlockSpec((MLP_FF, d_model), lambda i, j: (j, 0)),
                  _resident(g_post.shape)],
        out_specs=row_spec,
        scratch_shapes=[
            pltpu.VMEM((MLP_ROWS, d_model), jnp.bfloat16),
            pltpu.VMEM((MLP_ROWS, d_model), jnp.float32),
        ],
        compiler_params=pltpu.CompilerParams(
            dimension_semantics=("arbitrary", "arbitrary"),
            vmem_limit_bytes=VMEM_LIMIT_BYTES),
        name="mlp",
    )(h2d, g_pre, w_up, w_down, g_post)


def kernel(x, mix_pre_gain, w_in, conv_a_w, conv_b_w, conv_b_bias, ln_b_gain, ln_b_bias,
           w_out, mix_post_gain, mlp_pre_gain, w_up, w_down, mlp_post_gain):
    batch, seq, d_model = x.shape
    depth = w_in.shape[0]
    d_a = conv_a_w.shape[2]
    d_b = conv_b_w.shape[2]
    assert d_a == d_b and w_in.shape[2] == N_IN_PIECES * d_a
    assert conv_a_w.shape[1] - 1 <= HALO_A and conv_b_w.shape[1] - 1 <= HALO_B
    assert d_a % MIX_COLS == 0 and seq % MIX_ROWS == 0 and MIX_ROWS >= HALO_B
    assert MIX_ROWS % CONV_ROWS == 0 and MIX_ROWS % NORM_ROWS == 0
    assert (batch * seq) % MLP_ROWS == 0 and w_up.shape[2] % MLP_FF == 0
    n_mix_tiles = batch * seq // MIX_ROWS
    assert d_model % (2 * SUBLANES * n_mix_tiles) == 0
    assert w_up.shape[2] % (2 * SUBLANES * n_mix_tiles) == 0
    bf16 = jnp.bfloat16
    row = lambda a: a.reshape(1, -1)

    h = x.reshape(batch * seq, d_model)
    for l in range(depth):
        h, w_up_bf16, w_down_bf16 = _mixer(
            h, seq, row(mix_pre_gain[l]), w_in[l].astype(bf16),
            conv_a_w[l], conv_b_w[l], row(conv_b_bias[l]),
            row(ln_b_gain[l]), row(ln_b_bias[l]),
            w_out[l].astype(bf16), row(mix_post_gain[l]), w_up[l], w_down[l])
        h = _mlp(h, row(mlp_pre_gain[l]), w_up_bf16, w_down_bf16, row(mlp_post_gain[l]))
    return h.reshape(batch, seq, d_model)
```

```python
import functools

import jax
import jax.numpy as jnp
from jax import lax
from jax.experimental import pallas as pl
from jax.experimental.pallas import tpu as pltpu

EPS = 1e-6
N_IN_PIECES = 5

SUBLANES = 8
LANES = 128
MXU_WIDTH = 256

MIX_ROWS = 256
MIX_COLS = MXU_WIDTH
CONV_ROWS = 128
NORM_ROWS = 32
HALO_A = SUBLANES
HALO_B = 4 * SUBLANES
MLP_ROWS = 512
MLP_FF = 1024
VMEM_LIMIT_BYTES = 56 * 1024 * 1024


def _rms_scale(v):
    return lax.rsqrt(jnp.mean(v * v, axis=-1, keepdims=True) + EPS)


def _bits_fold(v):
    bits = pltpu.bitcast(v, jnp.uint32)
    rows = bits[0:SUBLANES]
    for r0 in range(SUBLANES, bits.shape[0], SUBLANES):
        rows = rows | bits[r0:r0 + SUBLANES]
    fold = rows[:, 0:LANES]
    for c0 in range(LANES, rows.shape[1], LANES):
        fold = fold | rows[:, c0:c0 + LANES]
    return fold


def _after(value, token):
    zero = ((token >> 16) >> 16).astype(jnp.float32)
    reps = (value.shape[0] // SUBLANES, value.shape[1] // LANES)
    return value + jnp.tile(zero, reps).astype(value.dtype)


def _causal_conv(ext_ref, w_ref, cols, rows, halo, init_ref, emit, on_first=None):
    width = w_ref.shape[0]
    first = halo - (width - 1)
    part_rows = CONV_ROWS + SUBLANES
    token = None
    for lane0 in range(cols.start, cols.stop, LANES):
        lanes = slice(lane0, lane0 + LANES)
        for row0 in range(0, rows, CONV_ROWS):
            if init_ref is None:
                acc = jnp.zeros((CONV_ROWS, LANES), jnp.float32)
            else:
                acc = jnp.broadcast_to(init_ref[:, lanes], (CONV_ROWS, LANES))
            for r in range(SUBLANES):
                taps = [s for s in range(r, halo + 1, SUBLANES) if 0 <= s - first < width]
                if not taps:
                    continue
                n = CONV_ROWS if r == 0 else part_rows
                part = None
                for s in taps:
                    a0 = row0 + s - r
                    term = w_ref[s - first:s - first + 1, lanes] * ext_ref[a0:a0 + n, lanes]
                    part = term if part is None else part + term
                if r == 0:
                    acc = acc + part
                else:
                    acc = acc + pltpu.roll(part, part_rows - r, axis=0)[0:CONV_ROWS]
            emit(slice(row0, row0 + CONV_ROWS), lanes, acc)
            fold = _bits_fold(acc)
            if token is None and on_first is not None:
                on_first(fold)
            token = fold if token is None else token | fold
    return token


def _mixer_kernel(x_ref, xp_ref, g_pre_ref, w_in_ref, caw_ref, cbw_ref, cbb_ref, lng_ref,
                  lnb_ref, w_out_ref, g_post_ref, w_up_ref, w_down_ref,
                  o_ref, w_up_bf16_ref, w_down_bf16_ref,
                  u0_s, u1_s, exta_s, extb_s, gc_s, oa_s, mixed_s,
                  *, d_a, d_b, tiles_per_seq):
    i = pl.program_id(0)
    rows = MIX_ROWS
    n_chunks = d_a // MIX_COLS

    @pl.when(i == 0)
    def _():
        exta_s[0:HALO_A, :] = jnp.zeros((HALO_A, d_a), jnp.float32)
        extb_s[...] = jnp.zeros(extb_s.shape, jnp.float32)
        mixed_s[...] = jnp.zeros(mixed_s.shape, jnp.bfloat16)

    seq_start = (i % tiles_per_seq) == 0
    exta_s[0:HALO_A, :] = jnp.where(seq_start, 0.0, exta_s[0:HALO_A, :])

    w_up_bf16_ref[...] = w_up_ref[...].astype(jnp.bfloat16)
    w_down_bf16_ref[...] = w_down_ref[...].astype(jnp.bfloat16)

    for r0 in range(0, rows, NORM_ROWS):
        xb = x_ref[r0:r0 + NORM_ROWS, :]
        u1_s[r0:r0 + NORM_ROWS, :] = (xb * _rms_scale(xb) * g_pre_ref[...]).astype(jnp.bfloat16)

    def start_out_proj(first_conv_block):
        oa_s[...] = jnp.dot(_after(mixed_s[:, 0:d_a], first_conv_block), w_out_ref[0:d_a, :],
                            preferred_element_type=jnp.float32)

    u_bufs = (u1_s, u0_s)
    for j in range(n_chunks):
        cs = slice(j * MIX_COLS, (j + 1) * MIX_COLS)

        def emit_b(rs, lanes, acc):
            gc_s[rs, lanes] = acc

        conv_done = _causal_conv(extb_s, cbw_ref, cs, rows, HALO_B, cbb_ref, emit_b,
                                 on_first=start_out_proj if j == 0 else None)

        u_src, u_dst = u_bufs[j % 2], u_bufs[(j + 1) % 2]
        u_dst[...] = _after(u_src[...], conv_done)
        b_gate, c_gate, v_a, val_b, gate_b = (
            jnp.dot(u_dst[...], w_in_ref[:, q * d_a + cs.start:q * d_a + cs.stop],
                    preferred_element_type=jnp.float32) for q in range(N_IN_PIECES))
        extb_s[0:HALO_B, cs] = jnp.where(seq_start, 0.0, extb_s[rows:rows + HALO_B, cs])
        extb_s[HALO_B:HALO_B + rows, cs] = val_b * jax.nn.sigmoid(gate_b)
        exta_s[HALO_A:HALO_A + rows, cs] = c_gate * v_a

        def emit_a(rs, lanes, acc, b_gate=b_gate, c0=cs.start):
            gate = b_gate[rs, lanes.start - c0:lanes.stop - c0]
            mixed_s[rs, lanes] = (gate * acc).astype(jnp.bfloat16)

        _causal_conv(exta_s, caw_ref, cs, rows, HALO_A, None, emit_a)
        exta_s[0:HALO_A, cs] = exta_s[rows:rows + HALO_A, cs]

    for r0 in range(0, rows, NORM_ROWS):
        g = gc_s[r0:r0 + NORM_ROWS, :]
        gd = g - jnp.mean(g, axis=-1, keepdims=True)
        var = jnp.mean(gd * gd, axis=-1, keepdims=True)
        y = gd * lax.rsqrt(var + EPS) * lng_ref[...] + lnb_ref[...]
        mixed_s[r0:r0 + NORM_ROWS, d_a:d_a + d_b] = (
            y * jax.nn.sigmoid(y)).astype(jnp.bfloat16)
    o = oa_s[...] + jnp.dot(mixed_s[:, d_a:d_a + d_b], w_out_ref[d_a:d_a + d_b, :],
                            preferred_element_type=jnp.float32)
    o_ref[...] = xp_ref[...] + o * _rms_scale(o) * g_post_ref[...]


def _mlp_kernel(h_ref, hp_ref, g_pre_ref, w_up_ref, w_down_ref, g_post_ref, o_ref,
                m_s, z_s, acc_s, *, n_ff):
    s = pl.program_id(0)
    j = s % n_ff
    rows = m_s.shape[0]

    @pl.when(s == 0)
    def _():
        z_s[...] = jnp.zeros(z_s.shape, jnp.bfloat16)
        acc_s[...] = jnp.zeros(acc_s.shape, jnp.float32)

    def up_projection():
        z = jnp.dot(m_s[...], w_up_ref[...], preferred_element_type=jnp.float32)
        z_s[...] = jnp.square(jnp.maximum(z, 0.0)).astype(jnp.bfloat16)

    @pl.when(j == 0)
    def _():
        first_block = None
        for r0 in range(0, rows, NORM_ROWS):
            hb = h_ref[r0:r0 + NORM_ROWS, :]
            mb = hb * _rms_scale(hb) * g_pre_ref[...]
            m_s[r0:r0 + NORM_ROWS, :] = mb.astype(jnp.bfloat16)
            if first_block is None:
                first_block = _bits_fold(mb)
        acc_s[...] += jnp.dot(_after(z_s[...], first_block), w_down_ref[...],
                              preferred_element_type=jnp.float32)
        for r0 in range(0, rows, NORM_ROWS):
            zz = acc_s[r0:r0 + NORM_ROWS, :]
            o_ref[r0:r0 + NORM_ROWS, :] = (hp_ref[r0:r0 + NORM_ROWS, :]
                                           + zz * _rms_scale(zz) * g_post_ref[...])
        up_projection()

    @pl.when(j != 0)
    def _():
        part = jnp.dot(z_s[...], w_down_ref[...], preferred_element_type=jnp.float32)
        acc_s[...] = jnp.where(j == 1, part, acc_s[...] + part)
        up_projection()


def _resident(shape):
    return pl.BlockSpec(shape, lambda *_: (0,) * len(shape), pipeline_mode=pl.Buffered(1))


def _mixer(h2d, seq, g_pre, w_in, caw, cbw, cbb, lng, lnb, w_out, g_post, w_up, w_down):
    n_rows, d_model = h2d.shape
    d_a = caw.shape[1]
    d_b = cbw.shape[1]
    d_ff = w_up.shape[1]
    n_tiles = n_rows // MIX_ROWS
    last = lambda i: jnp.minimum(i, n_tiles - 1)
    cur_spec = pl.BlockSpec((MIX_ROWS, d_model), lambda i: (last(i), 0))
    prev_spec = pl.BlockSpec((MIX_ROWS, d_model), lambda i: (jnp.maximum(i - 1, 0), 0))
    w_up_spec = pl.BlockSpec((d_model // n_tiles, d_ff), lambda i: (last(i), 0))
    w_down_spec = pl.BlockSpec((d_ff // n_tiles, d_model), lambda i: (last(i), 0))
    return pl.pallas_call(
        functools.partial(_mixer_kernel, d_a=d_a, d_b=d_b, tiles_per_seq=seq // MIX_ROWS),
        out_shape=(jax.ShapeDtypeStruct((n_rows, d_model), jnp.float32),
                   jax.ShapeDtypeStruct(w_up.shape, jnp.bfloat16),
                   jax.ShapeDtypeStruct(w_down.shape, jnp.bfloat16)),
        grid=(n_tiles + 1,),
        in_specs=[cur_spec, prev_spec, _resident(g_pre.shape), _resident(w_in.shape),
                  _resident(caw.shape), _resident(cbw.shape), _resident(cbb.shape),
                  _resident(lng.shape), _resident(lnb.shape), _resident(w_out.shape),
                  _resident(g_post.shape), w_up_spec, w_down_spec],
        out_specs=(prev_spec, w_up_spec, w_down_spec),
        scratch_shapes=[
            pltpu.VMEM((MIX_ROWS, d_model), jnp.bfloat16),
            pltpu.VMEM((MIX_ROWS, d_model), jnp.bfloat16),
            pltpu.VMEM((HALO_A + MIX_ROWS, d_a), jnp.float32),
            pltpu.VMEM((HALO_B + MIX_ROWS, d_b), jnp.float32),
            pltpu.VMEM((MIX_ROWS, d_b), jnp.float32),
            pltpu.VMEM((MIX_ROWS, d_model), jnp.float32),
            pltpu.VMEM((MIX_ROWS, d_a + d_b), jnp.bfloat16),
        ],
        compiler_params=pltpu.CompilerParams(
            dimension_semantics=("arbitrary",),
            vmem_limit_bytes=VMEM_LIMIT_BYTES),
        name="mixer",
    )(h2d, h2d, g_pre, w_in, caw, cbw, cbb, lng, lnb, w_out, g_post, w_up, w_down)


def _mlp(h2d, g_pre, w_up, w_down, g_post):
    n_rows, d_model = h2d.shape
    d_ff = w_up.shape[1]
    n_tiles = n_rows // MLP_ROWS
    n_ff = d_ff // MLP_FF
    assert n_ff >= 2
    tile = lambda s: s // n_ff
    cur_spec = pl.BlockSpec((MLP_ROWS, d_model), lambda s: (jnp.minimum(tile(s), n_tiles - 1), 0))
    prev_spec = pl.BlockSpec((MLP_ROWS, d_model), lambda s: (jnp.maximum(tile(s) - 1, 0), 0))
    out_spec = pl.BlockSpec(
        (MLP_ROWS, d_model),
        lambda s: (jnp.clip(tile(s) - jnp.where(s % n_ff == 0, 1, 0), 0, n_tiles - 1), 0))
    return pl.pallas_call(
        functools.partial(_mlp_kernel, n_ff=n_ff),
        out_shape=jax.ShapeDtypeStruct((n_rows, d_model), jnp.float32),
        grid=(n_tiles * n_ff + 1,),
        in_specs=[cur_spec, prev_spec, _resident(g_pre.shape),
                  pl.BlockSpec((d_model, MLP_FF), lambda s: (0, s % n_ff)),
                  pl.BlockSpec((MLP_FF, d_model), lambda s: ((s + n_ff - 1) % n_ff, 0)),
                  _resident(g_post.shape)],
        out_specs=out_spec,
        scratch_shapes=[
            pltpu.VMEM((MLP_ROWS, d_model), jnp.bfloat16),
            pltpu.VMEM((MLP_ROWS, MLP_FF), jnp.bfloat16),
            pltpu.VMEM((MLP_ROWS, d_model), jnp.float32),
        ],
        compiler_params=pltpu.CompilerParams(
            dimension_semantics=("arbitrary",),
            vmem_limit_bytes=VMEM_LIMIT_BYTES),
        name="mlp",
    )(h2d, h2d, g_pre, w_up, w_down, g_post)


def kernel(x, mix_pre_gain, w_in, conv_a_w, conv_b_w, conv_b_bias, ln_b_gain, ln_b_bias,
           w_out, mix_post_gain, mlp_pre_gain, w_up, w_down, mlp_post_gain):
    batch, seq, d_model = x.shape
    depth = w_in.shape[0]
    d_a = conv_a_w.shape[2]
    d_b = conv_b_w.shape[2]
    assert d_a == d_b and w_in.shape[2] == N_IN_PIECES * d_a
    assert conv_a_w.shape[1] - 1 <= HALO_A and conv_b_w.shape[1] - 1 <= HALO_B
    assert d_a % MIX_COLS == 0 and seq % MIX_ROWS == 0 and MIX_ROWS >= HALO_B
    assert MIX_ROWS % CONV_ROWS == 0 and MIX_ROWS % NORM_ROWS == 0
    assert (batch * seq) % MLP_ROWS == 0 and w_up.shape[2] % MLP_FF == 0
    assert MLP_ROWS % NORM_ROWS == 0
    n_mix_tiles = batch * seq // MIX_ROWS
    assert d_model % (2 * SUBLANES * n_mix_tiles) == 0
    assert w_up.shape[2] % (2 * SUBLANES * n_mix_tiles) == 0
    bf16 = jnp.bfloat16
    row = lambda a: a.reshape(1, -1)

    h = x.reshape(batch * seq, d_model)
    for l in range(depth):
        h, w_up_bf16, w_down_bf16 = _mixer(
            h, seq, row(mix_pre_gain[l]), w_in[l].astype(bf16),
            conv_a_w[l], conv_b_w[l], row(conv_b_bias[l]),
            row(ln_b_gain[l]), row(ln_b_bias[l]),
            w_out[l].astype(bf16), row(mix_post_gain[l]), w_up[l], w_down[l])
        h = _mlp(h, row(mlp_pre_gain[l]), w_up_bf16, w_down_bf16, row(mlp_post_gain[l]))
    return h.reshape(batch, seq, d_model)
```

```python
import functools

import jax
import jax.numpy as jnp
from jax import lax
from jax.experimental import pallas as pl
from jax.experimental.pallas import tpu as pltpu

EPS = 1e-6
N_IN_PIECES = 5

SUBLANES = 8
LANES = 128
MXU_WIDTH = 256

MIX_ROWS = 256
MIX_COLS = MXU_WIDTH
CONV_ROWS = 128
NORM_ROWS = 32
HALO_A = SUBLANES
HALO_B = 4 * SUBLANES
MLP_ROWS = 512
MLP_FF = 1024
VMEM_LIMIT_BYTES = 56 * 1024 * 1024


def _rms_scale(v):
    return lax.rsqrt(jnp.mean(v * v, axis=-1, keepdims=True) + EPS)


def _bits_fold(v):
    bits = pltpu.bitcast(v, jnp.uint32)
    rows = bits[0:SUBLANES]
    for r0 in range(SUBLANES, bits.shape[0], SUBLANES):
        rows = rows | bits[r0:r0 + SUBLANES]
    fold = rows[:, 0:LANES]
    for c0 in range(LANES, rows.shape[1], LANES):
        fold = fold | rows[:, c0:c0 + LANES]
    return fold


def _after(value, token):
    zero = ((token >> 16) >> 16).astype(jnp.float32)
    reps = (value.shape[0] // SUBLANES, value.shape[1] // LANES)
    return value + jnp.tile(zero, reps).astype(value.dtype)


def _causal_conv(ext_ref, w_ref, cols, rows, halo, init_ref, emit, on_first=None):
    width = w_ref.shape[0]
    first = halo - (width - 1)
    part_rows = CONV_ROWS + SUBLANES
    token = None
    for lane0 in range(cols.start, cols.stop, LANES):
        lanes = slice(lane0, lane0 + LANES)
        for row0 in range(0, rows, CONV_ROWS):
            if init_ref is None:
                acc = jnp.zeros((CONV_ROWS, LANES), jnp.float32)
            else:
                acc = jnp.broadcast_to(init_ref[:, lanes], (CONV_ROWS, LANES))
            for r in range(SUBLANES):
                taps = [s for s in range(r, halo + 1, SUBLANES) if 0 <= s - first < width]
                if not taps:
                    continue
                n = CONV_ROWS if r == 0 else part_rows
                part = None
                for s in taps:
                    a0 = row0 + s - r
                    term = w_ref[s - first:s - first + 1, lanes] * ext_ref[a0:a0 + n, lanes]
                    part = term if part is None else part + term
                if r == 0:
                    acc = acc + part
                else:
                    acc = acc + pltpu.roll(part, part_rows - r, axis=0)[0:CONV_ROWS]
            emit(slice(row0, row0 + CONV_ROWS), lanes, acc)
            fold = _bits_fold(acc)
            if token is None and on_first is not None:
                on_first(fold)
            token = fold if token is None else token | fold
    return token


def _mixer_kernel(x_ref, xp_ref, g_pre_ref, w_in_ref, caw_ref, cbw_ref, cbb_ref, lng_ref,
                  lnb_ref, w_out_ref, g_post_ref, w_up_ref, w_down_ref,
                  o_ref, w_up_bf16_ref, w_down_bf16_ref,
                  u0_s, u1_s, exta_s, extb_s, gc_s, oa_s, mixed_s,
                  *, d_a, d_b, tiles_per_seq):
    i = pl.program_id(0)
    rows = MIX_ROWS
    n_chunks = d_a // MIX_COLS

    @pl.when(i == 0)
    def _():
        exta_s[0:HALO_A, :] = jnp.zeros((HALO_A, d_a), jnp.float32)
        extb_s[...] = jnp.zeros(extb_s.shape, jnp.float32)
        mixed_s[...] = jnp.zeros(mixed_s.shape, jnp.bfloat16)

    seq_start = (i % tiles_per_seq) == 0
    exta_s[0:HALO_A, :] = jnp.where(seq_start, 0.0, exta_s[0:HALO_A, :])

    w_up_bf16_ref[...] = w_up_ref[...].astype(jnp.bfloat16)
    w_down_bf16_ref[...] = w_down_ref[...].astype(jnp.bfloat16)

    for r0 in range(0, rows, NORM_ROWS):
        xb = x_ref[r0:r0 + NORM_ROWS, :]
        u1_s[r0:r0 + NORM_ROWS, :] = (xb * _rms_scale(xb) * g_pre_ref[...]).astype(jnp.bfloat16)

    def start_out_proj(first_conv_block):
        oa_s[...] = jnp.dot(_after(mixed_s[:, 0:d_a], first_conv_block), w_out_ref[0:d_a, :],
                            preferred_element_type=jnp.float32)

    u_bufs = (u1_s, u0_s)
    for j in range(n_chunks):
        cs = slice(j * MIX_COLS, (j + 1) * MIX_COLS)

        def emit_b(rs, lanes, acc):
            gc_s[rs, lanes] = acc

        conv_done = _causal_conv(extb_s, cbw_ref, cs, rows, HALO_B, cbb_ref, emit_b,
                                 on_first=start_out_proj if j == 0 else None)

        u_src, u_dst = u_bufs[j % 2], u_bufs[(j + 1) % 2]
        u_dst[...] = _after(u_src[...], conv_done)
        b_gate, c_gate, v_a, val_b, gate_b = (
            jnp.dot(u_dst[...], w_in_ref[:, q * d_a + cs.start:q * d_a + cs.stop],
                    preferred_element_type=jnp.float32) for q in range(N_IN_PIECES))
        extb_s[0:HALO_B, cs] = jnp.where(seq_start, 0.0, extb_s[rows:rows + HALO_B, cs])
        extb_s[HALO_B:HALO_B + rows, cs] = val_b * jax.nn.sigmoid(gate_b)
        exta_s[HALO_A:HALO_A + rows, cs] = c_gate * v_a

        def emit_a(rs, lanes, acc, b_gate=b_gate, c0=cs.start):
            gate = b_gate[rs, lanes.start - c0:lanes.stop - c0]
            mixed_s[rs, lanes] = (gate * acc).astype(jnp.bfloat16)

        _causal_conv(exta_s, caw_ref, cs, rows, HALO_A, None, emit_a)
        exta_s[0:HALO_A, cs] = exta_s[rows:rows + HALO_A, cs]

    for r0 in range(0, rows, NORM_ROWS):
        g = gc_s[r0:r0 + NORM_ROWS, :]
        gd = g - jnp.mean(g, axis=-1, keepdims=True)
        var = jnp.mean(gd * gd, axis=-1, keepdims=True)
        y = gd * lax.rsqrt(var + EPS) * lng_ref[...] + lnb_ref[...]
        mixed_s[r0:r0 + NORM_ROWS, d_a:d_a + d_b] = (
            y * jax.nn.sigmoid(y)).astype(jnp.bfloat16)
    o = oa_s[...] + jnp.dot(mixed_s[:, d_a:d_a + d_b], w_out_ref[d_a:d_a + d_b, :],
                            preferred_element_type=jnp.float32)
    o_ref[...] = xp_ref[...] + o * _rms_scale(o) * g_post_ref[...]


def _mlp_kernel(h_ref, g_pre_ref, w_up_ref, w_down_ref, g_post_ref, o_ref, m_s, acc_s):
    j = pl.program_id(1)
    rows = m_s.shape[0]

    @pl.when(j == 0)
    def _():
        for r0 in range(0, rows, NORM_ROWS):
            hb = h_ref[r0:r0 + NORM_ROWS, :]
            m_s[r0:r0 + NORM_ROWS, :] = (hb * _rms_scale(hb) * g_pre_ref[...]).astype(jnp.bfloat16)

    z = jnp.dot(m_s[...], w_up_ref[...], preferred_element_type=jnp.float32)
    z = jnp.square(jnp.maximum(z, 0.0)).astype(jnp.bfloat16)
    part = jnp.dot(z, w_down_ref[...], preferred_element_type=jnp.float32)
    acc_s[...] = jnp.where(j == 0, part, acc_s[...] + part)

    @pl.when(j == pl.num_programs(1) - 1)
    def _():
        for r0 in range(0, rows, NORM_ROWS):
            zz = acc_s[r0:r0 + NORM_ROWS, :]
            o_ref[r0:r0 + NORM_ROWS, :] = (h_ref[r0:r0 + NORM_ROWS, :]
                                           + zz * _rms_scale(zz) * g_post_ref[...])


def _resident(shape):
    return pl.BlockSpec(shape, lambda *_: (0,) * len(shape), pipeline_mode=pl.Buffered(1))


def _mixer(h2d, seq, g_pre, w_in, caw, cbw, cbb, lng, lnb, w_out, g_post, w_up, w_down):
    n_rows, d_model = h2d.shape
    d_a = caw.shape[1]
    d_b = cbw.shape[1]
    d_ff = w_up.shape[1]
    n_tiles = n_rows // MIX_ROWS
    last = lambda i: jnp.minimum(i, n_tiles - 1)
    cur_spec = pl.BlockSpec((MIX_ROWS, d_model), lambda i: (last(i), 0))
    prev_spec = pl.BlockSpec((MIX_ROWS, d_model), lambda i: (jnp.maximum(i - 1, 0), 0))
    w_up_spec = pl.BlockSpec((d_model // n_tiles, d_ff), lambda i: (last(i), 0))
    w_down_spec = pl.BlockSpec((d_ff // n_tiles, d_model), lambda i: (last(i), 0))
    return pl.pallas_call(
        functools.partial(_mixer_kernel, d_a=d_a, d_b=d_b, tiles_per_seq=seq // MIX_ROWS),
        out_shape=(jax.ShapeDtypeStruct((n_rows, d_model), jnp.float32),
                   jax.ShapeDtypeStruct(w_up.shape, jnp.bfloat16),
                   jax.ShapeDtypeStruct(w_down.shape, jnp.bfloat16)),
        grid=(n_tiles + 1,),
        in_specs=[cur_spec, prev_spec, _resident(g_pre.shape), _resident(w_in.shape),
                  _resident(caw.shape), _resident(cbw.shape), _resident(cbb.shape),
                  _resident(lng.shape), _resident(lnb.shape), _resident(w_out.shape),
                  _resident(g_post.shape), w_up_spec, w_down_spec],
        out_specs=(prev_spec, w_up_spec, w_down_spec),
        scratch_shapes=[
            pltpu.VMEM((MIX_ROWS, d_model), jnp.bfloat16),
            pltpu.VMEM((MIX_ROWS, d_model), jnp.bfloat16),
            pltpu.VMEM((HALO_A + MIX_ROWS, d_a), jnp.float32),
            pltpu.VMEM((HALO_B + MIX_ROWS, d_b), jnp.float32),
            pltpu.VMEM((MIX_ROWS, d_b), jnp.float32),
            pltpu.VMEM((MIX_ROWS, d_model), jnp.float32),
            pltpu.VMEM((MIX_ROWS, d_a + d_b), jnp.bfloat16),
        ],
        compiler_params=pltpu.CompilerParams(
            dimension_semantics=("arbitrary",),
            vmem_limit_bytes=VMEM_LIMIT_BYTES),
        name="mixer",
    )(h2d, h2d, g_pre, w_in, caw, cbw, cbb, lng, lnb, w_out, g_post, w_up, w_down)


def _mlp(h2d, g_pre, w_up, w_down, g_post):
    n_rows, d_model = h2d.shape
    d_ff = w_up.shape[1]
    row_spec = pl.BlockSpec((MLP_ROWS, d_model), lambda i, j: (i, 0))
    return pl.pallas_call(
        _mlp_kernel,
        out_shape=jax.ShapeDtypeStruct((n_rows, d_model), jnp.float32),
        grid=(n_rows // MLP_ROWS, d_ff // MLP_FF),
        in_specs=[row_spec, _resident(g_pre.shape),
                  pl.BlockSpec((d_model, MLP_FF), lambda i, j: (0, j)),
                  pl.BlockSpec((MLP_FF, d_model), lambda i, j: (j, 0)),
                  _resident(g_post.shape)],
        out_specs=row_spec,
        scratch_shapes=[
            pltpu.VMEM((MLP_ROWS, d_model), jnp.bfloat16),
            pltpu.VMEM((MLP_ROWS, d_model), jnp.float32),
        ],
        compiler_params=pltpu.CompilerParams(
            dimension_semantics=("arbitrary", "arbitrary"),
            vmem_limit_bytes=VMEM_LIMIT_BYTES),
        name="mlp",
    )(h2d, g_pre, w_up, w_down, g_post)


def kernel(x, mix_pre_gain, w_in, conv_a_w, conv_b_w, conv_b_bias, ln_b_gain, ln_b_bias,
           w_out, mix_post_gain, mlp_pre_gain, w_up, w_down, mlp_post_gain):
    batch, seq, d_model = x.shape
    depth = w_in.shape[0]
    d_a = conv_a_w.shape[2]
    d_b = conv_b_w.shape[2]
    assert d_a == d_b and w_in.shape[2] == N_IN_PIECES * d_a
    assert conv_a_w.shape[1] - 1 <= HALO_A and conv_b_w.shape[1] - 1 <= HALO_B
    assert d_a % MIX_COLS == 0 and seq % MIX_ROWS == 0 and MIX_ROWS >= HALO_B
    assert MIX_ROWS % CONV_ROWS == 0 and MIX_ROWS % NORM_ROWS == 0
    assert (batch * seq) % MLP_ROWS == 0 and w_up.shape[2] % MLP_FF == 0
    assert MLP_ROWS % NORM_ROWS == 0
    n_mix_tiles = batch * seq // MIX_ROWS
    assert d_model % (2 * SUBLANES * n_mix_tiles) == 0
    assert w_up.shape[2] % (2 * SUBLANES * n_mix_tiles) == 0
    bf16 = jnp.bfloat16
    row = lambda a: a.reshape(1, -1)

    h = x.reshape(batch * seq, d_model)
    for l in range(depth):
        h, w_up_bf16, w_down_bf16 = _mixer(
            h, seq, row(mix_pre_gain[l]), w_in[l].astype(bf16),
            conv_a_w[l], conv_b_w[l], row(conv_b_bias[l]),
            row(ln_b_gain[l]), row(ln_b_bias[l]),
            w_out[l].astype(bf16), row(mix_post_gain[l]), w_up[l], w_down[l])
        h = _mlp(h, row(mlp_pre_gain[l]), w_up_bf16, w_down_bf16, row(mlp_post_gain[l]))
    return h.reshape(batch, seq, d_model)
```

```python
import functools

import jax
import jax.numpy as jnp
from jax import lax
from jax.experimental import pallas as pl
from jax.experimental.pallas import tpu as pltpu

EPS = 1e-6
N_IN_PIECES = 5

SUBLANES = 8
LANES = 128
MXU_WIDTH = 256

MIX_ROWS = 256
MIX_COLS = MXU_WIDTH
CONV_ROWS = 64
NORM_ROWS = 32
HALO_A = SUBLANES
HALO_B = 4 * SUBLANES
MLP_ROWS = 512
MLP_FF = 1024
VMEM_LIMIT_BYTES = 56 * 1024 * 1024


def _rms_scale(v):
    return lax.rsqrt(jnp.mean(v * v, axis=-1, keepdims=True) + EPS)


def _bits_fold(v):
    bits = pltpu.bitcast(v, jnp.uint32)
    rows = bits[0:SUBLANES]
    for r0 in range(SUBLANES, bits.shape[0], SUBLANES):
        rows = rows | bits[r0:r0 + SUBLANES]
    fold = rows[:, 0:LANES]
    for c0 in range(LANES, rows.shape[1], LANES):
        fold = fold | rows[:, c0:c0 + LANES]
    return fold


def _after(value, token):
    zero = ((token >> 16) >> 16).astype(jnp.float32)
    reps = (value.shape[0] // SUBLANES, value.shape[1] // LANES)
    return value + jnp.tile(zero, reps).astype(value.dtype)


def _causal_conv(ext_ref, w_ref, cols, rows, halo, init_ref, emit, on_first=None):
    width = w_ref.shape[0]
    first = halo - (width - 1)
    part_rows = CONV_ROWS + SUBLANES
    token = None
    for lane0 in range(cols.start, cols.stop, LANES):
        lanes = slice(lane0, lane0 + LANES)
        for row0 in range(0, rows, CONV_ROWS):
            if init_ref is None:
                acc = jnp.zeros((CONV_ROWS, LANES), jnp.float32)
            else:
                acc = jnp.broadcast_to(init_ref[:, lanes], (CONV_ROWS, LANES))
            for r in range(SUBLANES):
                taps = [s for s in range(r, halo + 1, SUBLANES) if 0 <= s - first < width]
                if not taps:
                    continue
                n = CONV_ROWS if r == 0 else part_rows
                part = None
                for s in taps:
                    a0 = row0 + s - r
                    term = w_ref[s - first:s - first + 1, lanes] * ext_ref[a0:a0 + n, lanes]
                    part = term if part is None else part + term
                if r == 0:
                    acc = acc + part
                else:
                    acc = acc + pltpu.roll(part, part_rows - r, axis=0)[0:CONV_ROWS]
            emit(slice(row0, row0 + CONV_ROWS), lanes, acc)
            fold = _bits_fold(acc)
            if token is None and on_first is not None:
                on_first(fold)
            token = fold if token is None else token | fold
    return token


def _mixer_kernel(x_ref, xp_ref, g_pre_ref, w_in_ref, caw_ref, cbw_ref, cbb_ref, lng_ref,
                  lnb_ref, w_out_ref, g_post_ref, w_up_ref, w_down_ref,
                  o_ref, w_up_bf16_ref, w_down_bf16_ref,
                  u0_s, u1_s, exta_s, extb_s, gc_s, oa_s, mixed_s,
                  *, d_a, d_b, tiles_per_seq):
    i = pl.program_id(0)
    rows = MIX_ROWS
    n_chunks = d_a // MIX_COLS

    @pl.when(i == 0)
    def _():
        exta_s[0:HALO_A, :] = jnp.zeros((HALO_A, d_a), jnp.float32)
        extb_s[...] = jnp.zeros(extb_s.shape, jnp.float32)
        mixed_s[...] = jnp.zeros(mixed_s.shape, jnp.bfloat16)

    seq_start = (i % tiles_per_seq) == 0
    exta_s[0:HALO_A, :] = jnp.where(seq_start, 0.0, exta_s[0:HALO_A, :])

    w_up_bf16_ref[...] = w_up_ref[...].astype(jnp.bfloat16)
    w_down_bf16_ref[...] = w_down_ref[...].astype(jnp.bfloat16)

    for r0 in range(0, rows, NORM_ROWS):
        xb = x_ref[r0:r0 + NORM_ROWS, :]
        u1_s[r0:r0 + NORM_ROWS, :] = (xb * _rms_scale(xb) * g_pre_ref[...]).astype(jnp.bfloat16)

    def start_out_proj(first_conv_block):
        oa_s[...] = jnp.dot(_after(mixed_s[:, 0:d_a], first_conv_block), w_out_ref[0:d_a, :],
                            preferred_element_type=jnp.float32)

    u_bufs = (u1_s, u0_s)
    for j in range(n_chunks):
        cs = slice(j * MIX_COLS, (j + 1) * MIX_COLS)

        def emit_b(rs, lanes, acc):
            gc_s[rs, lanes] = acc

        conv_done = _causal_conv(extb_s, cbw_ref, cs, rows, HALO_B, cbb_ref, emit_b,
                                 on_first=start_out_proj if j == 0 else None)

        u_src, u_dst = u_bufs[j % 2], u_bufs[(j + 1) % 2]
        u_dst[...] = _after(u_src[...], conv_done)
        b_gate, c_gate, v_a, val_b, gate_b = (
            jnp.dot(u_dst[...], w_in_ref[:, q * d_a + cs.start:q * d_a + cs.stop],
                    preferred_element_type=jnp.float32) for q in range(N_IN_PIECES))
        extb_s[0:HALO_B, cs] = jnp.where(seq_start, 0.0, extb_s[rows:rows + HALO_B, cs])
        extb_s[HALO_B:HALO_B + rows, cs] = val_b * jax.nn.sigmoid(gate_b)
        exta_s[HALO_A:HALO_A + rows, cs] = c_gate * v_a

        def emit_a(rs, lanes, acc, b_gate=b_gate, c0=cs.start):
            gate = b_gate[rs, lanes.start - c0:lanes.stop - c0]
            mixed_s[rs, lanes] = (gate * acc).astype(jnp.bfloat16)

        _causal_conv(exta_s, caw_ref, cs, rows, HALO_A, None, emit_a)
        exta_s[0:HALO_A, cs] = exta_s[rows:rows + HALO_A, cs]

    for r0 in range(0, rows, NORM_ROWS):
        g = gc_s[r0:r0 + NORM_ROWS, :]
        gd = g - jnp.mean(g, axis=-1, keepdims=True)
        var = jnp.mean(gd * gd, axis=-1, keepdims=True)
        y = gd * lax.rsqrt(var + EPS) * lng_ref[...] + lnb_ref[...]
        mixed_s[r0:r0 + NORM_ROWS, d_a:d_a + d_b] = (
            y * jax.nn.sigmoid(y)).astype(jnp.bfloat16)
    o = oa_s[...] + jnp.dot(mixed_s[:, d_a:d_a + d_b], w_out_ref[d_a:d_a + d_b, :],
                            preferred_element_type=jnp.float32)
    o_ref[...] = xp_ref[...] + o * _rms_scale(o) * g_post_ref[...]


def _mlp_kernel(h_ref, g_pre_ref, w_up_ref, w_down_ref, g_post_ref, o_ref, m_s, acc_s):
    j = pl.program_id(1)
    rows = m_s.shape[0]

    @pl.when(j == 0)
    def _():
        for r0 in range(0, rows, NORM_ROWS):
            hb = h_ref[r0:r0 + NORM_ROWS, :]
            m_s[r0:r0 + NORM_ROWS, :] = (hb * _rms_scale(hb) * g_pre_ref[...]).astype(jnp.bfloat16)

    z = jnp.dot(m_s[...], w_up_ref[...], preferred_element_type=jnp.float32)
    z = jnp.square(jnp.maximum(z, 0.0)).astype(jnp.bfloat16)
    part = jnp.dot(z, w_down_ref[...], preferred_element_type=jnp.float32)
    acc_s[...] = jnp.where(j == 0, part, acc_s[...] + part)

    @pl.when(j == pl.num_programs(1) - 1)
    def _():
        for r0 in range(0, rows, NORM_ROWS):
            zz = acc_s[r0:r0 + NORM_ROWS, :]
            o_ref[r0:r0 + NORM_ROWS, :] = (h_ref[r0:r0 + NORM_ROWS, :]
                                           + zz * _rms_scale(zz) * g_post_ref[...])


def _resident(shape):
    return pl.BlockSpec(shape, lambda *_: (0,) * len(shape), pipeline_mode=pl.Buffered(1))


def _mixer(h2d, seq, g_pre, w_in, caw, cbw, cbb, lng, lnb, w_out, g_post, w_up, w_down):
    n_rows, d_model = h2d.shape
    d_a = caw.shape[1]
    d_b = cbw.shape[1]
    d_ff = w_up.shape[1]
    n_tiles = n_rows // MIX_ROWS
    last = lambda i: jnp.minimum(i, n_tiles - 1)
    cur_spec = pl.BlockSpec((MIX_ROWS, d_model), lambda i: (last(i), 0))
    prev_spec = pl.BlockSpec((MIX_ROWS, d_model), lambda i: (jnp.maximum(i - 1, 0), 0))
    w_up_spec = pl.BlockSpec((d_model // n_tiles, d_ff), lambda i: (last(i), 0))
    w_down_spec = pl.BlockSpec((d_ff // n_tiles, d_model), lambda i: (last(i), 0))
    return pl.pallas_call(
        functools.partial(_mixer_kernel, d_a=d_a, d_b=d_b, tiles_per_seq=seq // MIX_ROWS),
        out_shape=(jax.ShapeDtypeStruct((n_rows, d_model), jnp.float32),
                   jax.ShapeDtypeStruct(w_up.shape, jnp.bfloat16),
                   jax.ShapeDtypeStruct(w_down.shape, jnp.bfloat16)),
        grid=(n_tiles + 1,),
        in_specs=[cur_spec, prev_spec, _resident(g_pre.shape), _resident(w_in.shape),
                  _resident(caw.shape), _resident(cbw.shape), _resident(cbb.shape),
                  _resident(lng.shape), _resident(lnb.shape), _resident(w_out.shape),
                  _resident(g_post.shape), w_up_spec, w_down_spec],
        out_specs=(prev_spec, w_up_spec, w_down_spec),
        scratch_shapes=[
            pltpu.VMEM((MIX_ROWS, d_model), jnp.bfloat16),
            pltpu.VMEM((MIX_ROWS, d_model), jnp.bfloat16),
            pltpu.VMEM((HALO_A + MIX_ROWS, d_a), jnp.float32),
            pltpu.VMEM((HALO_B + MIX_ROWS, d_b), jnp.float32),
            pltpu.VMEM((MIX_ROWS, d_b), jnp.float32),
            pltpu.VMEM((MIX_ROWS, d_model), jnp.float32),
            pltpu.VMEM((MIX_ROWS, d_a + d_b), jnp.bfloat16),
        ],
        compiler_params=pltpu.CompilerParams(
            dimension_semantics=("arbitrary",),
            vmem_limit_bytes=VMEM_LIMIT_BYTES),
        name="mixer",
    )(h2d, h2d, g_pre, w_in, caw, cbw, cbb, lng, lnb, w_out, g_post, w_up, w_down)


def _mlp(h2d, g_pre, w_up, w_down, g_post):
    n_rows, d_model = h2d.shape
    d_ff = w_up.shape[1]
    row_spec = pl.BlockSpec((MLP_ROWS, d_model), lambda i, j: (i, 0))
    return pl.pallas_call(
        _mlp_kernel,
        out_shape=jax.ShapeDtypeStruct((n_rows, d_model), jnp.float32),
        grid=(n_rows // MLP_ROWS, d_ff // MLP_FF),
        in_specs=[row_spec, _resident(g_pre.shape),
                  pl.BlockSpec((d_model, MLP_FF), lambda i, j: (0, j)),
                  pl.BlockSpec((MLP_FF, d_model), lambda i, j: (j, 0)),
                  _resident(g_post.shape)],
        out_specs=row_spec,
        scratch_shapes=[
            pltpu.VMEM((MLP_ROWS, d_model), jnp.bfloat16),
            pltpu.VMEM((MLP_ROWS, d_model), jnp.float32),
        ],
        compiler_params=pltpu.CompilerParams(
            dimension_semantics=("arbitrary", "arbitrary"),
            vmem_limit_bytes=VMEM_LIMIT_BYTES),
        name="mlp",
    )(h2d, g_pre, w_up, w_down, g_post)


def kernel(x, mix_pre_gain, w_in, conv_a_w, conv_b_w, conv_b_bias, ln_b_gain, ln_b_bias,
           w_out, mix_post_gain, mlp_pre_gain, w_up, w_down, mlp_post_gain):
    batch, seq, d_model = x.shape
    depth = w_in.shape[0]
    d_a = conv_a_w.shape[2]
    d_b = conv_b_w.shape[2]
    assert d_a == d_b and w_in.shape[2] == N_IN_PIECES * d_a
    assert conv_a_w.shape[1] - 1 <= HALO_A and conv_b_w.shape[1] - 1 <= HALO_B
    assert d_a % MIX_COLS == 0 and seq % MIX_ROWS == 0 and MIX_ROWS >= HALO_B
    assert MIX_ROWS % CONV_ROWS == 0 and MIX_ROWS % NORM_ROWS == 0
    assert (batch * seq) % MLP_ROWS == 0 and w_up.shape[2] % MLP_FF == 0
    assert MLP_ROWS % NORM_ROWS == 0
    n_mix_tiles = batch * seq // MIX_ROWS
    assert d_model % (2 * SUBLANES * n_mix_tiles) == 0
    assert w_up.shape[2] % (2 * SUBLANES * n_mix_tiles) == 0
    bf16 = jnp.bfloat16
    row = lambda a: a.reshape(1, -1)

    h = x.reshape(batch * seq, d_model)
    for l in range(depth):
        h, w_up_bf16, w_down_bf16 = _mixer(
            h, seq, row(mix_pre_gain[l]), w_in[l].astype(bf16),
            conv_a_w[l], conv_b_w[l], row(conv_b_bias[l]),
            row(ln_b_gain[l]), row(ln_b_bias[l]),
            w_out[l].astype(bf16), row(mix_post_gain[l]), w_up[l], w_down[l])
        h = _mlp(h, row(mlp_pre_gain[l]), w_up_bf16, w_down_bf16, row(mlp_post_gain[l]))
    return h.reshape(batch, seq, d_model)
```

```python
import functools

import jax
import jax.numpy as jnp
from jax import lax
from jax.experimental import pallas as pl
from jax.experimental.pallas import tpu as pltpu

EPS = 1e-6
N_IN_PIECES = 5

SUBLANES = 8
LANES = 128
MXU_WIDTH = 256

MIX_ROWS = 256
MIX_COLS = MXU_WIDTH
CONV_ROWS = 64
NORM_ROWS = 32
HALO_A = SUBLANES
HALO_B = 4 * SUBLANES
MLP_ROWS = 512
MLP_FF = 1024
VMEM_LIMIT_BYTES = 56 * 1024 * 1024


def _rms_scale(v):
    return lax.rsqrt(jnp.mean(v * v, axis=-1, keepdims=True) + EPS)


def _bits_fold(v):
    bits = pltpu.bitcast(v, jnp.uint32)
    rows = bits[0:SUBLANES]
    for r0 in range(SUBLANES, bits.shape[0], SUBLANES):
        rows = rows | bits[r0:r0 + SUBLANES]
    fold = rows[:, 0:LANES]
    for c0 in range(LANES, rows.shape[1], LANES):
        fold = fold | rows[:, c0:c0 + LANES]
    return fold


def _after(value, token):
    zero = ((token >> 16) >> 16).astype(jnp.float32)
    reps = (value.shape[0] // SUBLANES, value.shape[1] // LANES)
    return value + jnp.tile(zero, reps).astype(value.dtype)


def _causal_conv(ext_ref, w_ref, cols, rows, halo, init_ref, emit, on_first=None):
    width = w_ref.shape[0]
    first = halo - (width - 1)
    part_rows = CONV_ROWS + SUBLANES
    token = None
    for lane0 in range(cols.start, cols.stop, LANES):
        lanes = slice(lane0, lane0 + LANES)
        for row0 in range(0, rows, CONV_ROWS):
            if init_ref is None:
                acc = jnp.zeros((CONV_ROWS, LANES), jnp.float32)
            else:
                acc = jnp.broadcast_to(init_ref[:, lanes], (CONV_ROWS, LANES))
            for r in range(SUBLANES):
                taps = [s for s in range(r, halo + 1, SUBLANES) if 0 <= s - first < width]
                if not taps:
                    continue
                n = CONV_ROWS if r == 0 else part_rows
                part = None
                for s in taps:
                    a0 = row0 + s - r
                    term = w_ref[s - first:s - first + 1, lanes] * ext_ref[a0:a0 + n, lanes]
                    part = term if part is None else part + term
                if r == 0:
                    acc = acc + part
                else:
                    acc = acc + pltpu.roll(part, part_rows - r, axis=0)[0:CONV_ROWS]
            emit(slice(row0, row0 + CONV_ROWS), lanes, acc)
            fold = _bits_fold(acc)
            if token is None and on_first is not None:
                on_first(fold)
            token = fold if token is None else token | fold
    return token


def _mixer_kernel(x_ref, xp_ref, gains_ref, w_in_ref, caw_ref, cbw_ref, vecs_ref, w_out_ref,
                  w_up_ref, w_down_ref,
                  o_ref, w_up_bf16_ref, w_down_bf16_ref,
                  u0_s, u1_s, exta_s, extb_s, gc_s, oa_s, mixed_s,
                  *, d_a, d_b, tiles_per_seq):
    i = pl.program_id(0)
    rows = MIX_ROWS
    n_chunks = d_a // MIX_COLS
    g_pre, g_post = gains_ref[0:1, :], gains_ref[1:2, :]
    conv_bias_ref = vecs_ref.at[0:1, :]
    ln_gain, ln_bias = vecs_ref[1:2, :], vecs_ref[2:3, :]

    @pl.when(i == 0)
    def _():
        exta_s[0:HALO_A, :] = jnp.zeros((HALO_A, d_a), jnp.float32)
        extb_s[...] = jnp.zeros(extb_s.shape, jnp.float32)
        mixed_s[...] = jnp.zeros(mixed_s.shape, jnp.bfloat16)

    seq_start = (i % tiles_per_seq) == 0
    exta_s[0:HALO_A, :] = jnp.where(seq_start, 0.0, exta_s[0:HALO_A, :])

    w_up_bf16_ref[...] = w_up_ref[...].astype(jnp.bfloat16)
    w_down_bf16_ref[...] = w_down_ref[...].astype(jnp.bfloat16)

    for r0 in range(0, rows, NORM_ROWS):
        xb = x_ref[r0:r0 + NORM_ROWS, :]
        u1_s[r0:r0 + NORM_ROWS, :] = (xb * _rms_scale(xb) * g_pre).astype(jnp.bfloat16)

    def start_out_proj(first_conv_block):
        oa_s[...] = jnp.dot(_after(mixed_s[:, 0:d_a], first_conv_block), w_out_ref[0:d_a, :],
                            preferred_element_type=jnp.float32)

    u_bufs = (u1_s, u0_s)
    for j in range(n_chunks):
        cs = slice(j * MIX_COLS, (j + 1) * MIX_COLS)

        def emit_b(rs, lanes, acc):
            gc_s[rs, lanes] = acc

        conv_done = _causal_conv(extb_s, cbw_ref, cs, rows, HALO_B, conv_bias_ref, emit_b,
                                 on_first=start_out_proj if j == 0 else None)

        u_src, u_dst = u_bufs[j % 2], u_bufs[(j + 1) % 2]
        u_dst[...] = _after(u_src[...], conv_done)
        b_gate, c_gate, v_a, val_b, gate_b = (
            jnp.dot(u_dst[...], w_in_ref[:, q * d_a + cs.start:q * d_a + cs.stop],
                    preferred_element_type=jnp.float32) for q in range(N_IN_PIECES))
        extb_s[0:HALO_B, cs] = jnp.where(seq_start, 0.0, extb_s[rows:rows + HALO_B, cs])
        extb_s[HALO_B:HALO_B + rows, cs] = val_b * jax.nn.sigmoid(gate_b)
        exta_s[HALO_A:HALO_A + rows, cs] = c_gate * v_a

        def emit_a(rs, lanes, acc, b_gate=b_gate, c0=cs.start):
            gate = b_gate[rs, lanes.start - c0:lanes.stop - c0]
            mixed_s[rs, lanes] = (gate * acc).astype(jnp.bfloat16)

        _causal_conv(exta_s, caw_ref, cs, rows, HALO_A, None, emit_a)
        exta_s[0:HALO_A, cs] = exta_s[rows:rows + HALO_A, cs]

    for r0 in range(0, rows, NORM_ROWS):
        g = gc_s[r0:r0 + NORM_ROWS, :]
        gd = g - jnp.mean(g, axis=-1, keepdims=True)
        var = jnp.mean(gd * gd, axis=-1, keepdims=True)
        y = gd * lax.rsqrt(var + EPS) * ln_gain + ln_bias
        mixed_s[r0:r0 + NORM_ROWS, d_a:d_a + d_b] = (
            y * jax.nn.sigmoid(y)).astype(jnp.bfloat16)
    o = oa_s[...] + jnp.dot(mixed_s[:, d_a:d_a + d_b], w_out_ref[d_a:d_a + d_b, :],
                            preferred_element_type=jnp.float32)
    o_ref[...] = xp_ref[...] + o * _rms_scale(o) * g_post


def _mlp_kernel(h_ref, gains_ref, w_up_ref, w_down_ref, o_ref, m_s, acc_s):
    j = pl.program_id(1)
    rows = m_s.shape[0]
    g_pre, g_post = gains_ref[0:1, :], gains_ref[1:2, :]

    @pl.when(j == 0)
    def _():
        for r0 in range(0, rows, NORM_ROWS):
            hb = h_ref[r0:r0 + NORM_ROWS, :]
            m_s[r0:r0 + NORM_ROWS, :] = (hb * _rms_scale(hb) * g_pre).astype(jnp.bfloat16)

    z = jnp.dot(m_s[...], w_up_ref[...], preferred_element_type=jnp.float32)
    z = jnp.square(jnp.maximum(z, 0.0)).astype(jnp.bfloat16)
    part = jnp.dot(z, w_down_ref[...], preferred_element_type=jnp.float32)
    acc_s[...] = jnp.where(j == 0, part, acc_s[...] + part)

    @pl.when(j == pl.num_programs(1) - 1)
    def _():
        for r0 in range(0, rows, NORM_ROWS):
            zz = acc_s[r0:r0 + NORM_ROWS, :]
            o_ref[r0:r0 + NORM_ROWS, :] = (h_ref[r0:r0 + NORM_ROWS, :]
                                           + zz * _rms_scale(zz) * g_post)


def _resident(shape):
    return pl.BlockSpec(shape, lambda *_: (0,) * len(shape), pipeline_mode=pl.Buffered(1))


def _mixer(h2d, seq, gains, w_in, caw, cbw, vecs, w_out, w_up, w_down):
    n_rows, d_model = h2d.shape
    d_a = caw.shape[1]
    d_b = cbw.shape[1]
    d_ff = w_up.shape[1]
    n_tiles = n_rows // MIX_ROWS
    last = lambda i: jnp.minimum(i, n_tiles - 1)
    cur_spec = pl.BlockSpec((MIX_ROWS, d_model), lambda i: (last(i), 0))
    prev_spec = pl.BlockSpec((MIX_ROWS, d_model), lambda i: (jnp.maximum(i - 1, 0), 0))
    w_up_spec = pl.BlockSpec((d_model // n_tiles, d_ff), lambda i: (last(i), 0))
    w_down_spec = pl.BlockSpec((d_ff // n_tiles, d_model), lambda i: (last(i), 0))
    return pl.pallas_call(
        functools.partial(_mixer_kernel, d_a=d_a, d_b=d_b, tiles_per_seq=seq // MIX_ROWS),
        out_shape=(jax.ShapeDtypeStruct((n_rows, d_model), jnp.float32),
                   jax.ShapeDtypeStruct(w_up.shape, jnp.bfloat16),
                   jax.ShapeDtypeStruct(w_down.shape, jnp.bfloat16)),
        grid=(n_tiles + 1,),
        in_specs=[cur_spec, prev_spec, _resident(gains.shape), _resident(w_in.shape),
                  _resident(caw.shape), _resident(cbw.shape), _resident(vecs.shape),
                  _resident(w_out.shape), w_up_spec, w_down_spec],
        out_specs=(prev_spec, w_up_spec, w_down_spec),
        scratch_shapes=[
            pltpu.VMEM((MIX_ROWS, d_model), jnp.bfloat16),
            pltpu.VMEM((MIX_ROWS, d_model), jnp.bfloat16),
            pltpu.VMEM((HALO_A + MIX_ROWS, d_a), jnp.float32),
            pltpu.VMEM((HALO_B + MIX_ROWS, d_b), jnp.float32),
            pltpu.VMEM((MIX_ROWS, d_b), jnp.float32),
            pltpu.VMEM((MIX_ROWS, d_model), jnp.float32),
            pltpu.VMEM((MIX_ROWS, d_a + d_b), jnp.bfloat16),
        ],
        compiler_params=pltpu.CompilerParams(
            dimension_semantics=("arbitrary",),
            vmem_limit_bytes=VMEM_LIMIT_BYTES),
        name="mixer",
    )(h2d, h2d, gains, w_in, caw, cbw, vecs, w_out, w_up, w_down)


def _mlp(h2d, gains, w_up, w_down):
    n_rows, d_model = h2d.shape
    d_ff = w_up.shape[1]
    row_spec = pl.BlockSpec((MLP_ROWS, d_model), lambda i, j: (i, 0))
    return pl.pallas_call(
        _mlp_kernel,
        out_shape=jax.ShapeDtypeStruct((n_rows, d_model), jnp.float32),
        grid=(n_rows // MLP_ROWS, d_ff // MLP_FF),
        in_specs=[row_spec, _resident(gains.shape),
                  pl.BlockSpec((d_model, MLP_FF), lambda i, j: (0, j)),
                  pl.BlockSpec((MLP_FF, d_model), lambda i, j: (j, 0))],
        out_specs=row_spec,
        scratch_shapes=[
            pltpu.VMEM((MLP_ROWS, d_model), jnp.bfloat16),
            pltpu.VMEM((MLP_ROWS, d_model), jnp.float32),
        ],
        compiler_params=pltpu.CompilerParams(
            dimension_semantics=("arbitrary", "arbitrary"),
            vmem_limit_bytes=VMEM_LIMIT_BYTES),
        name="mlp",
    )(h2d, gains, w_up, w_down)


def _rows(*vectors):
    pad = jnp.zeros((SUBLANES - len(vectors), vectors[0].shape[0]), jnp.float32)
    return jnp.concatenate([jnp.stack(vectors).astype(jnp.float32), pad])


def kernel(x, mix_pre_gain, w_in, conv_a_w, conv_b_w, conv_b_bias, ln_b_gain, ln_b_bias,
           w_out, mix_post_gain, mlp_pre_gain, w_up, w_down, mlp_post_gain):
    batch, seq, d_model = x.shape
    depth = w_in.shape[0]
    d_a = conv_a_w.shape[2]
    d_b = conv_b_w.shape[2]
    assert d_a == d_b and w_in.shape[2] == N_IN_PIECES * d_a
    assert conv_a_w.shape[1] - 1 <= HALO_A and conv_b_w.shape[1] - 1 <= HALO_B
    assert d_a % MIX_COLS == 0 and seq % MIX_ROWS == 0 and MIX_ROWS >= HALO_B
    assert MIX_ROWS % CONV_ROWS == 0 and MIX_ROWS % NORM_ROWS == 0
    assert (batch * seq) % MLP_ROWS == 0 and w_up.shape[2] % MLP_FF == 0
    assert MLP_ROWS % NORM_ROWS == 0
    n_mix_tiles = batch * seq // MIX_ROWS
    assert d_model % (2 * SUBLANES * n_mix_tiles) == 0
    assert w_up.shape[2] % (2 * SUBLANES * n_mix_tiles) == 0
    bf16 = jnp.bfloat16

    h = x.reshape(batch * seq, d_model)
    for l in range(depth):
        h, w_up_bf16, w_down_bf16 = _mixer(
            h, seq, _rows(mix_pre_gain[l], mix_post_gain[l]), w_in[l].astype(bf16),
            conv_a_w[l], conv_b_w[l], _rows(conv_b_bias[l], ln_b_gain[l], ln_b_bias[l]),
            w_out[l].astype(bf16), w_up[l], w_down[l])
        h = _mlp(h, _rows(mlp_pre_gain[l], mlp_post_gain[l]), w_up_bf16, w_down_bf16)
    return h.reshape(batch, seq, d_model)
```

```python
import functools

import jax
import jax.numpy as jnp
from jax import lax
from jax.experimental import pallas as pl
from jax.experimental.pallas import tpu as pltpu

EPS = 1e-6
N_IN_PIECES = 5

SUBLANES = 8
LANES = 128
MXU_WIDTH = 256

MIX_ROWS = 256
MIX_COLS = MXU_WIDTH
CONV_ROWS = 64
NORM_ROWS = 32
HALO_A = SUBLANES
HALO_B = 4 * SUBLANES
MLP_ROWS = 512
MLP_FF = 1024
MLP_OUT_ROWS = 16
VMEM_LIMIT_BYTES = 56 * 1024 * 1024


def _rms_scale(v):
    return lax.rsqrt(jnp.mean(v * v, axis=-1, keepdims=True) + EPS)


def _bits_fold(v):
    bits = pltpu.bitcast(v, jnp.uint32)
    rows = bits[0:SUBLANES]
    for r0 in range(SUBLANES, bits.shape[0], SUBLANES):
        rows = rows | bits[r0:r0 + SUBLANES]
    fold = rows[:, 0:LANES]
    for c0 in range(LANES, rows.shape[1], LANES):
        fold = fold | rows[:, c0:c0 + LANES]
    return fold


def _after(value, token):
    zero = ((token >> 16) >> 16).astype(jnp.float32)
    reps = (value.shape[0] // SUBLANES, value.shape[1] // LANES)
    return value + jnp.tile(zero, reps).astype(value.dtype)


def _causal_conv(ext_ref, w_ref, cols, rows, halo, init_ref, emit, on_first=None):
    width = w_ref.shape[0]
    first = halo - (width - 1)
    part_rows = CONV_ROWS + SUBLANES
    token = None
    for lane0 in range(cols.start, cols.stop, LANES):
        lanes = slice(lane0, lane0 + LANES)
        for row0 in range(0, rows, CONV_ROWS):
            if init_ref is None:
                acc = jnp.zeros((CONV_ROWS, LANES), jnp.float32)
            else:
                acc = jnp.broadcast_to(init_ref[:, lanes], (CONV_ROWS, LANES))
            for r in range(SUBLANES):
                taps = [s for s in range(r, halo + 1, SUBLANES) if 0 <= s - first < width]
                if not taps:
                    continue
                n = CONV_ROWS if r == 0 else part_rows
                part = None
                for s in taps:
                    a0 = row0 + s - r
                    term = w_ref[s - first:s - first + 1, lanes] * ext_ref[a0:a0 + n, lanes]
                    part = term if part is None else part + term
                if r == 0:
                    acc = acc + part
                else:
                    acc = acc + pltpu.roll(part, part_rows - r, axis=0)[0:CONV_ROWS]
            emit(slice(row0, row0 + CONV_ROWS), lanes, acc)
            fold = _bits_fold(acc)
            if token is None and on_first is not None:
                on_first(fold)
            token = fold if token is None else token | fold
    return token


def _mixer_kernel(x_ref, xp_ref, g_pre_ref, w_in_ref, caw_ref, cbw_ref, cbb_ref, lng_ref,
                  lnb_ref, w_out_ref, g_post_ref, w_up_ref, w_down_ref,
                  o_ref, w_up_bf16_ref, w_down_bf16_ref,
                  u0_s, u1_s, exta_s, extb_s, gc_s, oa_s, mixed_s,
                  *, d_a, d_b, tiles_per_seq):
    i = pl.program_id(0)
    rows = MIX_ROWS
    n_chunks = d_a // MIX_COLS

    @pl.when(i == 0)
    def _():
        exta_s[0:HALO_A, :] = jnp.zeros((HALO_A, d_a), jnp.float32)
        extb_s[...] = jnp.zeros(extb_s.shape, jnp.float32)
        mixed_s[...] = jnp.zeros(mixed_s.shape, jnp.bfloat16)

    seq_start = (i % tiles_per_seq) == 0
    exta_s[0:HALO_A, :] = jnp.where(seq_start, 0.0, exta_s[0:HALO_A, :])

    w_up_bf16_ref[...] = w_up_ref[...].astype(jnp.bfloat16)
    w_down_bf16_ref[...] = w_down_ref[...].astype(jnp.bfloat16)

    for r0 in range(0, rows, NORM_ROWS):
        xb = x_ref[r0:r0 + NORM_ROWS, :]
        u1_s[r0:r0 + NORM_ROWS, :] = (xb * _rms_scale(xb) * g_pre_ref[...]).astype(jnp.bfloat16)

    def start_out_proj(first_conv_block):
        oa_s[...] = jnp.dot(_after(mixed_s[:, 0:d_a], first_conv_block), w_out_ref[0:d_a, :],
                            preferred_element_type=jnp.float32)

    u_bufs = (u1_s, u0_s)
    for j in range(n_chunks):
        cs = slice(j * MIX_COLS, (j + 1) * MIX_COLS)

        def emit_b(rs, lanes, acc):
            gc_s[rs, lanes] = acc

        conv_done = _causal_conv(extb_s, cbw_ref, cs, rows, HALO_B, cbb_ref, emit_b,
                                 on_first=start_out_proj if j == 0 else None)

        u_src, u_dst = u_bufs[j % 2], u_bufs[(j + 1) % 2]
        u_dst[...] = _after(u_src[...], conv_done)
        b_gate, c_gate, v_a, val_b, gate_b = (
            jnp.dot(u_dst[...], w_in_ref[:, q * d_a + cs.start:q * d_a + cs.stop],
                    preferred_element_type=jnp.float32) for q in range(N_IN_PIECES))
        extb_s[0:HALO_B, cs] = jnp.where(seq_start, 0.0, extb_s[rows:rows + HALO_B, cs])
        extb_s[HALO_B:HALO_B + rows, cs] = val_b * jax.nn.sigmoid(gate_b)
        exta_s[HALO_A:HALO_A + rows, cs] = c_gate * v_a

        def emit_a(rs, lanes, acc, b_gate=b_gate, c0=cs.start):
            gate = b_gate[rs, lanes.start - c0:lanes.stop - c0]
            mixed_s[rs, lanes] = (gate * acc).astype(jnp.bfloat16)

        _causal_conv(exta_s, caw_ref, cs, rows, HALO_A, None, emit_a)
        exta_s[0:HALO_A, cs] = exta_s[rows:rows + HALO_A, cs]

    for r0 in range(0, rows, NORM_ROWS):
        g = gc_s[r0:r0 + NORM_ROWS, :]
        gd = g - jnp.mean(g, axis=-1, keepdims=True)
        var = jnp.mean(gd * gd, axis=-1, keepdims=True)
        y = gd * lax.rsqrt(var + EPS) * lng_ref[...] + lnb_ref[...]
        mixed_s[r0:r0 + NORM_ROWS, d_a:d_a + d_b] = (
            y * jax.nn.sigmoid(y)).astype(jnp.bfloat16)
    o = oa_s[...] + jnp.dot(mixed_s[:, d_a:d_a + d_b], w_out_ref[d_a:d_a + d_b, :],
                            preferred_element_type=jnp.float32)
    o_ref[...] = xp_ref[...] + o * _rms_scale(o) * g_post_ref[...]


def _mlp_kernel(h_ref, h_next_ref, g_pre_ref, w_up_ref, w_down_ref, g_post_ref, o_ref,
                m_s, acc_s, *, n_ff):
    i = pl.program_id(0)
    j = pl.program_id(1)
    rows = acc_s.shape[0]
    slot = i % 2

    @pl.when((i == 0) & (j == 0))
    def _():
        for r0 in range(0, rows, NORM_ROWS):
            hb = h_ref[r0:r0 + NORM_ROWS, :]
            m_s[0, r0:r0 + NORM_ROWS, :] = (
                hb * _rms_scale(hb) * g_pre_ref[...]).astype(jnp.bfloat16)

    z = jnp.dot(m_s[slot], w_up_ref[...], preferred_element_type=jnp.float32)
    z = jnp.square(jnp.maximum(z, 0.0)).astype(jnp.bfloat16)
    part = jnp.dot(z, w_down_ref[...], preferred_element_type=jnp.float32)
    acc_s[...] = jnp.where(j == 0, part, acc_s[...] + part)

    piece = rows // n_ff
    base = pl.multiple_of(j * piece, piece)
    for r0 in range(0, piece, NORM_ROWS):
        hb = h_next_ref[pl.ds(base + r0, NORM_ROWS), :]
        m_s[1 - slot, pl.ds(base + r0, NORM_ROWS), :] = (
            hb * _rms_scale(hb) * g_pre_ref[...]).astype(jnp.bfloat16)

    @pl.when(j == n_ff - 1)
    def _():
        for r0 in range(0, rows, MLP_OUT_ROWS):
            zz = acc_s[r0:r0 + MLP_OUT_ROWS, :]
            o_ref[r0:r0 + MLP_OUT_ROWS, :] = (h_ref[r0:r0 + MLP_OUT_ROWS, :]
                                              + zz * _rms_scale(zz) * g_post_ref[...])


def _resident(shape):
    return pl.BlockSpec(shape, lambda *_: (0,) * len(shape), pipeline_mode=pl.Buffered(1))


def _mixer(h2d, seq, g_pre, w_in, caw, cbw, cbb, lng, lnb, w_out, g_post, w_up, w_down):
    n_rows, d_model = h2d.shape
    d_a = caw.shape[1]
    d_b = cbw.shape[1]
    d_ff = w_up.shape[1]
    n_tiles = n_rows // MIX_ROWS
    last = lambda i: jnp.minimum(i, n_tiles - 1)
    cur_spec = pl.BlockSpec((MIX_ROWS, d_model), lambda i: (last(i), 0))
    prev_spec = pl.BlockSpec((MIX_ROWS, d_model), lambda i: (jnp.maximum(i - 1, 0), 0))
    w_up_spec = pl.BlockSpec((d_model // n_tiles, d_ff), lambda i: (last(i), 0))
    w_down_spec = pl.BlockSpec((d_ff // n_tiles, d_model), lambda i: (last(i), 0))
    return pl.pallas_call(
        functools.partial(_mixer_kernel, d_a=d_a, d_b=d_b, tiles_per_seq=seq // MIX_ROWS),
        out_shape=(jax.ShapeDtypeStruct((n_rows, d_model), jnp.float32),
                   jax.ShapeDtypeStruct(w_up.shape, jnp.bfloat16),
                   jax.ShapeDtypeStruct(w_down.shape, jnp.bfloat16)),
        grid=(n_tiles + 1,),
        in_specs=[cur_spec, prev_spec, _resident(g_pre.shape), _resident(w_in.shape),
                  _resident(caw.shape), _resident(cbw.shape), _resident(cbb.shape),
                  _resident(lng.shape), _resident(lnb.shape), _resident(w_out.shape),
                  _resident(g_post.shape), w_up_spec, w_down_spec],
        out_specs=(prev_spec, w_up_spec, w_down_spec),
        scratch_shapes=[
            pltpu.VMEM((MIX_ROWS, d_model), jnp.bfloat16),
            pltpu.VMEM((MIX_ROWS, d_model), jnp.bfloat16),
            pltpu.VMEM((HALO_A + MIX_ROWS, d_a), jnp.float32),
            pltpu.VMEM((HALO_B + MIX_ROWS, d_b), jnp.float32),
            pltpu.VMEM((MIX_ROWS, d_b), jnp.float32),
            pltpu.VMEM((MIX_ROWS, d_model), jnp.float32),
            pltpu.VMEM((MIX_ROWS, d_a + d_b), jnp.bfloat16),
        ],
        compiler_params=pltpu.CompilerParams(
            dimension_semantics=("arbitrary",),
            vmem_limit_bytes=VMEM_LIMIT_BYTES),
        name="mixer",
    )(h2d, h2d, g_pre, w_in, caw, cbw, cbb, lng, lnb, w_out, g_post, w_up, w_down)


def _mlp(h2d, g_pre, w_up, w_down, g_post):
    n_rows, d_model = h2d.shape
    d_ff = w_up.shape[1]
    n_tiles = n_rows // MLP_ROWS
    n_ff = d_ff // MLP_FF
    assert MLP_ROWS % (n_ff * NORM_ROWS) == 0 and MLP_ROWS % MLP_OUT_ROWS == 0
    row_spec = pl.BlockSpec((MLP_ROWS, d_model), lambda i, j: (i, 0))
    next_spec = pl.BlockSpec((MLP_ROWS, d_model),
                             lambda i, j: (jnp.minimum(i + 1, n_tiles - 1), 0))
    return pl.pallas_call(
        functools.partial(_mlp_kernel, n_ff=n_ff),
        out_shape=jax.ShapeDtypeStruct((n_rows, d_model), jnp.float32),
        grid=(n_tiles, n_ff),
        in_specs=[row_spec, next_spec, _resident(g_pre.shape),
                  pl.BlockSpec((d_model, MLP_FF), lambda i, j: (0, j)),
                  pl.BlockSpec((MLP_FF, d_model), lambda i, j: (j, 0)),
                  _resident(g_post.shape)],
        out_specs=row_spec,
        scratch_shapes=[
            pltpu.VMEM((2, MLP_ROWS, d_model), jnp.bfloat16),
            pltpu.VMEM((MLP_ROWS, d_model), jnp.float32),
        ],
        compiler_params=pltpu.CompilerParams(
            dimension_semantics=("arbitrary", "arbitrary"),
            vmem_limit_bytes=VMEM_LIMIT_BYTES),
        name="mlp",
    )(h2d, h2d, g_pre, w_up, w_down, g_post)


def kernel(x, mix_pre_gain, w_in, conv_a_w, conv_b_w, conv_b_bias, ln_b_gain, ln_b_bias,
           w_out, mix_post_gain, mlp_pre_gain, w_up, w_down, mlp_post_gain):
    batch, seq, d_model = x.shape
    depth = w_in.shape[0]
    d_a = conv_a_w.shape[2]
    d_b = conv_b_w.shape[2]
    assert d_a == d_b and w_in.shape[2] == N_IN_PIECES * d_a
    assert conv_a_w.shape[1] - 1 <= HALO_A and conv_b_w.shape[1] - 1 <= HALO_B
    assert d_a % MIX_COLS == 0 and seq % MIX_ROWS == 0 and MIX_ROWS >= HALO_B
    assert MIX_ROWS % CONV_ROWS == 0 and MIX_ROWS % NORM_ROWS == 0
    assert (batch * seq) % MLP_ROWS == 0 and w_up.shape[2] % MLP_FF == 0
    assert MLP_ROWS % NORM_ROWS == 0
    n_mix_tiles = batch * seq // MIX_ROWS
    assert d_model % (2 * SUBLANES * n_mix_tiles) == 0
    assert w_up.shape[2] % (2 * SUBLANES * n_mix_tiles) == 0
    bf16 = jnp.bfloat16
    row = lambda a: a.reshape(1, -1)

    h = x.reshape(batch * seq, d_model)
    for l in range(depth):
        h, w_up_bf16, w_down_bf16 = _mixer(
            h, seq, row(mix_pre_gain[l]), w_in[l].astype(bf16),
            conv_a_w[l], conv_b_w[l], row(conv_b_bias[l]),
            row(ln_b_gain[l]), row(ln_b_bias[l]),
            w_out[l].astype(bf16), row(mix_post_gain[l]), w_up[l], w_down[l])
        h = _mlp(h, row(mlp_pre_gain[l]), w_up_bf16, w_down_bf16, row(mlp_post_gain[l]))
    return h.reshape(batch, seq, d_model)
```

```python
import functools

import jax
import jax.numpy as jnp
from jax import lax
from jax.experimental import pallas as pl
from jax.experimental.pallas import tpu as pltpu

EPS = 1e-6
N_IN_PIECES = 5

SUBLANES = 8
LANES = 128
MXU_WIDTH = 256

MIX_ROWS = 256
MIX_COLS = MXU_WIDTH
CONV_ROWS = 64
NORM_ROWS = 32
HALO_A = SUBLANES
HALO_B = 4 * SUBLANES
MLP_ROWS = 1024
MLP_FF = 512
VMEM_LIMIT_BYTES = 56 * 1024 * 1024
MLP_VMEM_LIMIT_BYTES = 60 * 1024 * 1024


def _rms_scale(v):
    return lax.rsqrt(jnp.mean(v * v, axis=-1, keepdims=True) + EPS)


def _bits_fold(v):
    bits = pltpu.bitcast(v, jnp.uint32)
    rows = bits[0:SUBLANES]
    for r0 in range(SUBLANES, bits.shape[0], SUBLANES):
        rows = rows | bits[r0:r0 + SUBLANES]
    fold = rows[:, 0:LANES]
    for c0 in range(LANES, rows.shape[1], LANES):
        fold = fold | rows[:, c0:c0 + LANES]
    return fold


def _after(value, token):
    zero = ((token >> 16) >> 16).astype(jnp.float32)
    reps = (value.shape[0] // SUBLANES, value.shape[1] // LANES)
    return value + jnp.tile(zero, reps).astype(value.dtype)


def _causal_conv(ext_ref, w_ref, cols, rows, halo, init_ref, emit, on_first=None):
    width = w_ref.shape[0]
    first = halo - (width - 1)
    part_rows = CONV_ROWS + SUBLANES
    token = None
    for lane0 in range(cols.start, cols.stop, LANES):
        lanes = slice(lane0, lane0 + LANES)
        for row0 in range(0, rows, CONV_ROWS):
            if init_ref is None:
                acc = jnp.zeros((CONV_ROWS, LANES), jnp.float32)
            else:
                acc = jnp.broadcast_to(init_ref[:, lanes], (CONV_ROWS, LANES))
            for r in range(SUBLANES):
                taps = [s for s in range(r, halo + 1, SUBLANES) if 0 <= s - first < width]
                if not taps:
                    continue
                n = CONV_ROWS if r == 0 else part_rows
                part = None
                for s in taps:
                    a0 = row0 + s - r
                    term = w_ref[s - first:s - first + 1, lanes] * ext_ref[a0:a0 + n, lanes]
                    part = term if part is None else part + term
                if r == 0:
                    acc = acc + part
                else:
                    acc = acc + pltpu.roll(part, part_rows - r, axis=0)[0:CONV_ROWS]
            emit(slice(row0, row0 + CONV_ROWS), lanes, acc)
            fold = _bits_fold(acc)
            if token is None and on_first is not None:
                on_first(fold)
            token = fold if token is None else token | fold
    return token


def _mixer_kernel(x_ref, xp_ref, g_pre_ref, w_in_ref, caw_ref, cbw_ref, cbb_ref, lng_ref,
                  lnb_ref, w_out_ref, g_post_ref, w_up_ref, w_down_ref,
                  o_ref, w_up_bf16_ref, w_down_bf16_ref,
                  u0_s, u1_s, exta_s, extb_s, gc_s, oa_s, mixed_s,
                  *, d_a, d_b, tiles_per_seq):
    i = pl.program_id(0)
    rows = MIX_ROWS
    n_chunks = d_a // MIX_COLS

    @pl.when(i == 0)
    def _():
        exta_s[0:HALO_A, :] = jnp.zeros((HALO_A, d_a), jnp.float32)
        extb_s[...] = jnp.zeros(extb_s.shape, jnp.float32)
        mixed_s[...] = jnp.zeros(mixed_s.shape, jnp.bfloat16)

    seq_start = (i % tiles_per_seq) == 0
    exta_s[0:HALO_A, :] = jnp.where(seq_start, 0.0, exta_s[0:HALO_A, :])

    w_up_bf16_ref[...] = w_up_ref[...].astype(jnp.bfloat16)
    w_down_bf16_ref[...] = w_down_ref[...].astype(jnp.bfloat16)

    for r0 in range(0, rows, NORM_ROWS):
        xb = x_ref[r0:r0 + NORM_ROWS, :]
        u1_s[r0:r0 + NORM_ROWS, :] = (xb * _rms_scale(xb) * g_pre_ref[...]).astype(jnp.bfloat16)

    def start_out_proj(first_conv_block):
        oa_s[...] = jnp.dot(_after(mixed_s[:, 0:d_a], first_conv_block), w_out_ref[0:d_a, :],
                            preferred_element_type=jnp.float32)

    u_bufs = (u1_s, u0_s)
    for j in range(n_chunks):
        cs = slice(j * MIX_COLS, (j + 1) * MIX_COLS)

        def emit_b(rs, lanes, acc):
            gc_s[rs, lanes] = acc

        conv_done = _causal_conv(extb_s, cbw_ref, cs, rows, HALO_B, cbb_ref, emit_b,
                                 on_first=start_out_proj if j == 0 else None)

        u_src, u_dst = u_bufs[j % 2], u_bufs[(j + 1) % 2]
        u_dst[...] = _after(u_src[...], conv_done)
        b_gate, c_gate, v_a, val_b, gate_b = (
            jnp.dot(u_dst[...], w_in_ref[:, q * d_a + cs.start:q * d_a + cs.stop],
                    preferred_element_type=jnp.float32) for q in range(N_IN_PIECES))
        extb_s[0:HALO_B, cs] = jnp.where(seq_start, 0.0, extb_s[rows:rows + HALO_B, cs])
        extb_s[HALO_B:HALO_B + rows, cs] = val_b * jax.nn.sigmoid(gate_b)
        exta_s[HALO_A:HALO_A + rows, cs] = c_gate * v_a

        def emit_a(rs, lanes, acc, b_gate=b_gate, c0=cs.start):
            gate = b_gate[rs, lanes.start - c0:lanes.stop - c0]
            mixed_s[rs, lanes] = (gate * acc).astype(jnp.bfloat16)

        _causal_conv(exta_s, caw_ref, cs, rows, HALO_A, None, emit_a)
        exta_s[0:HALO_A, cs] = exta_s[rows:rows + HALO_A, cs]

    for r0 in range(0, rows, NORM_ROWS):
        g = gc_s[r0:r0 + NORM_ROWS, :]
        gd = g - jnp.mean(g, axis=-1, keepdims=True)
        var = jnp.mean(gd * gd, axis=-1, keepdims=True)
        y = gd * lax.rsqrt(var + EPS) * lng_ref[...] + lnb_ref[...]
        mixed_s[r0:r0 + NORM_ROWS, d_a:d_a + d_b] = (
            y * jax.nn.sigmoid(y)).astype(jnp.bfloat16)
    o = oa_s[...] + jnp.dot(mixed_s[:, d_a:d_a + d_b], w_out_ref[d_a:d_a + d_b, :],
                            preferred_element_type=jnp.float32)
    o_ref[...] = xp_ref[...] + o * _rms_scale(o) * g_post_ref[...]


def _mlp_kernel(h_ref, g_pre_ref, w_up_ref, w_down_ref, g_post_ref, o_ref, m_s, acc_s):
    j = pl.program_id(1)
    rows = m_s.shape[0]

    @pl.when(j == 0)
    def _():
        for r0 in range(0, rows, NORM_ROWS):
            hb = h_ref[r0:r0 + NORM_ROWS, :]
            m_s[r0:r0 + NORM_ROWS, :] = (hb * _rms_scale(hb) * g_pre_ref[...]).astype(jnp.bfloat16)

    z = jnp.dot(m_s[...], w_up_ref[...], preferred_element_type=jnp.float32)
    z = jnp.square(jnp.maximum(z, 0.0)).astype(jnp.bfloat16)
    part = jnp.dot(z, w_down_ref[...], preferred_element_type=jnp.float32)
    acc_s[...] = jnp.where(j == 0, part, acc_s[...] + part)

    @pl.when(j == pl.num_programs(1) - 1)
    def _():
        for r0 in range(0, rows, NORM_ROWS):
            zz = acc_s[r0:r0 + NORM_ROWS, :]
            o_ref[r0:r0 + NORM_ROWS, :] = (h_ref[r0:r0 + NORM_ROWS, :]
                                           + zz * _rms_scale(zz) * g_post_ref[...])


def _resident(shape):
    return pl.BlockSpec(shape, lambda *_: (0,) * len(shape), pipeline_mode=pl.Buffered(1))


def _mixer(h2d, seq, g_pre, w_in, caw, cbw, cbb, lng, lnb, w_out, g_post, w_up, w_down):
    n_rows, d_model = h2d.shape
    d_a = caw.shape[1]
    d_b = cbw.shape[1]
    d_ff = w_up.shape[1]
    n_tiles = n_rows // MIX_ROWS
    last = lambda i: jnp.minimum(i, n_tiles - 1)
    cur_spec = pl.BlockSpec((MIX_ROWS, d_model), lambda i: (last(i), 0))
    prev_spec = pl.BlockSpec((MIX_ROWS, d_model), lambda i: (jnp.maximum(i - 1, 0), 0))
    w_up_spec = pl.BlockSpec((d_model // n_tiles, d_ff), lambda i: (last(i), 0))
    w_down_spec = pl.BlockSpec((d_ff // n_tiles, d_model), lambda i: (last(i), 0))
    return pl.pallas_call(
        functools.partial(_mixer_kernel, d_a=d_a, d_b=d_b, tiles_per_seq=seq // MIX_ROWS),
        out_shape=(jax.ShapeDtypeStruct((n_rows, d_model), jnp.float32),
                   jax.ShapeDtypeStruct(w_up.shape, jnp.bfloat16),
                   jax.ShapeDtypeStruct(w_down.shape, jnp.bfloat16)),
        grid=(n_tiles + 1,),
        in_specs=[cur_spec, prev_spec, _resident(g_pre.shape), _resident(w_in.shape),
                  _resident(caw.shape), _resident(cbw.shape), _resident(cbb.shape),
                  _resident(lng.shape), _resident(lnb.shape), _resident(w_out.shape),
                  _resident(g_post.shape), w_up_spec, w_down_spec],
        out_specs=(prev_spec, w_up_spec, w_down_spec),
        scratch_shapes=[
            pltpu.VMEM((MIX_ROWS, d_model), jnp.bfloat16),
            pltpu.VMEM((MIX_ROWS, d_model), jnp.bfloat16),
            pltpu.VMEM((HALO_A + MIX_ROWS, d_a), jnp.float32),
            pltpu.VMEM((HALO_B + MIX_ROWS, d_b), jnp.float32),
            pltpu.VMEM((MIX_ROWS, d_b), jnp.float32),
            pltpu.VMEM((MIX_ROWS, d_model), jnp.float32),
            pltpu.VMEM((MIX_ROWS, d_a + d_b), jnp.bfloat16),
        ],
        compiler_params=pltpu.CompilerParams(
            dimension_semantics=("arbitrary",),
            vmem_limit_bytes=VMEM_LIMIT_BYTES),
        name="mixer",
    )(h2d, h2d, g_pre, w_in, caw, cbw, cbb, lng, lnb, w_out, g_post, w_up, w_down)


def _mlp(h2d, g_pre, w_up, w_down, g_post):
    n_rows, d_model = h2d.shape
    d_ff = w_up.shape[1]
    row_spec = pl.BlockSpec((MLP_ROWS, d_model), lambda i, j: (i, 0))
    return pl.pallas_call(
        _mlp_kernel,
        out_shape=jax.ShapeDtypeStruct((n_rows, d_model), jnp.float32),
        grid=(n_rows // MLP_ROWS, d_ff // MLP_FF),
        in_specs=[row_spec, _resident(g_pre.shape),
                  pl.BlockSpec((d_model, MLP_FF), lambda i, j: (0, j)),
                  pl.BlockSpec((MLP_FF, d_model), lambda i, j: (j, 0)),
                  _resident(g_post.shape)],
        out_specs=row_spec,
        scratch_shapes=[
            pltpu.VMEM((MLP_ROWS, d_model), jnp.bfloat16),
            pltpu.VMEM((MLP_ROWS, d_model), jnp.float32),
        ],
        compiler_params=pltpu.CompilerParams(
            dimension_semantics=("arbitrary", "arbitrary"),
            vmem_limit_bytes=MLP_VMEM_LIMIT_BYTES),
        name="mlp",
    )(h2d, g_pre, w_up, w_down, g_post)


def kernel(x, mix_pre_gain, w_in, conv_a_w, conv_b_w, conv_b_bias, ln_b_gain, ln_b_bias,
           w_out, mix_post_gain, mlp_pre_gain, w_up, w_down, mlp_post_gain):
    batch, seq, d_model = x.shape
    depth = w_in.shape[0]
    d_a = conv_a_w.shape[2]
    d_b = conv_b_w.shape[2]
    assert d_a == d_b and w_in.shape[2] == N_IN_PIECES * d_a
    assert conv_a_w.shape[1] - 1 <= HALO_A and conv_b_w.shape[1] - 1 <= HALO_B
    assert d_a % MIX_COLS == 0 and seq % MIX_ROWS == 0 and MIX_ROWS >= HALO_B
    assert MIX_ROWS % CONV_ROWS == 0 and MIX_ROWS % NORM_ROWS == 0
    assert (batch * seq) % MLP_ROWS == 0 and w_up.shape[2] % MLP_FF == 0
    assert MLP_ROWS % NORM_ROWS == 0
    n_mix_tiles = batch * seq // MIX_ROWS
    assert d_model % (2 * SUBLANES * n_mix_tiles) == 0
    assert w_up.shape[2] % (2 * SUBLANES * n_mix_tiles) == 0
    bf16 = jnp.bfloat16
    row = lambda a: a.reshape(1, -1)

    h = x.reshape(batch * seq, d_model)
    for l in range(depth):
        h, w_up_bf16, w_down_bf16 = _mixer(
            h, seq, row(mix_pre_gain[l]), w_in[l].astype(bf16),
            conv_a_w[l], conv_b_w[l], row(conv_b_bias[l]),
            row(ln_b_gain[l]), row(ln_b_bias[l]),
            w_out[l].astype(bf16), row(mix_post_gain[l]), w_up[l], w_down[l])
        h = _mlp(h, row(mlp_pre_gain[l]), w_up_bf16, w_down_bf16, row(mlp_post_gain[l]))
    return h.reshape(batch, seq, d_model)
```

```python
import functools

import jax
import jax.numpy as jnp
from jax import lax
from jax.experimental import pallas as pl
from jax.experimental.pallas import tpu as pltpu

EPS = 1e-6
N_IN_PIECES = 5

SUBLANES = 8
LANES = 128
MXU_WIDTH = 256

MIX_ROWS = 256
MIX_COLS = MXU_WIDTH
CONV_ROWS = 64
NORM_ROWS = 32
HALO_A = SUBLANES
HALO_B = 4 * SUBLANES
MLP_ROWS = 512
MLP_FF = 1024
VMEM_LIMIT_BYTES = 56 * 1024 * 1024


def _rms_scale(v):
    return lax.rsqrt(jnp.mean(v * v, axis=-1, keepdims=True) + EPS)


def _bits_fold(v):
    bits = pltpu.bitcast(v, jnp.uint32)
    rows = bits[0:SUBLANES]
    for r0 in range(SUBLANES, bits.shape[0], SUBLANES):
        rows = rows | bits[r0:r0 + SUBLANES]
    fold = rows[:, 0:LANES]
    for c0 in range(LANES, rows.shape[1], LANES):
        fold = fold | rows[:, c0:c0 + LANES]
    return fold


def _after(value, token):
    zero = ((token >> 16) >> 16).astype(jnp.float32)
    reps = (value.shape[0] // SUBLANES, value.shape[1] // LANES)
    return value + jnp.tile(zero, reps).astype(value.dtype)


def _causal_conv(ext_ref, w_ref, cols, rows, halo, init_ref, emit, on_first=None):
    width = w_ref.shape[0]
    first = halo - (width - 1)
    part_rows = CONV_ROWS + SUBLANES
    token = None
    for lane0 in range(cols.start, cols.stop, LANES):
        lanes = slice(lane0, lane0 + LANES)
        for row0 in range(0, rows, CONV_ROWS):
            if init_ref is None:
                acc = jnp.zeros((CONV_ROWS, LANES), jnp.float32)
            else:
                acc = jnp.broadcast_to(init_ref[:, lanes], (CONV_ROWS, LANES))
            for r in range(SUBLANES):
                taps = [s for s in range(r, halo + 1, SUBLANES) if 0 <= s - first < width]
                if not taps:
                    continue
                n = CONV_ROWS if r == 0 else part_rows
                part = None
                for s in taps:
                    a0 = row0 + s - r
                    term = w_ref[s - first:s - first + 1, lanes] * ext_ref[a0:a0 + n, lanes]
                    part = term if part is None else part + term
                if r == 0:
                    acc = acc + part
                else:
                    acc = acc + pltpu.roll(part, part_rows - r, axis=0)[0:CONV_ROWS]
            emit(slice(row0, row0 + CONV_ROWS), lanes, acc)
            fold = _bits_fold(acc)
            if token is None and on_first is not None:
                on_first(fold)
            token = fold if token is None else token | fold
    return token


def _mixer_kernel(x_ref, xp_ref, g_pre_ref, w_in_ref, caw_ref, cbw_ref, cbb_ref, lng_ref,
                  lnb_ref, w_out_ref, g_post_ref, w_up_ref, w_down_ref,
                  o_ref, w_up_bf16_ref, w_down_bf16_ref,
                  u0_s, u1_s, exta_s, extb_s, gc_s, oa_s, mixed_s,
                  *, d_a, d_b, tiles_per_seq):
    i = pl.program_id(0)
    rows = MIX_ROWS
    n_chunks = d_a // MIX_COLS

    @pl.when(i == 0)
    def _():
        exta_s[0:HALO_A, :] = jnp.zeros((HALO_A, d_a), jnp.float32)
        extb_s[...] = jnp.zeros(extb_s.shape, jnp.float32)
        mixed_s[...] = jnp.zeros(mixed_s.shape, jnp.bfloat16)

    seq_start = (i % tiles_per_seq) == 0
    exta_s[0:HALO_A, :] = jnp.where(seq_start, 0.0, exta_s[0:HALO_A, :])

    w_up_bf16_ref[...] = w_up_ref[...].astype(jnp.bfloat16)
    w_down_bf16_ref[...] = w_down_ref[...].astype(jnp.bfloat16)

    for r0 in range(0, rows, NORM_ROWS):
        xb = x_ref[r0:r0 + NORM_ROWS, :]
        u1_s[r0:r0 + NORM_ROWS, :] = (xb * _rms_scale(xb) * g_pre_ref[...]).astype(jnp.bfloat16)

    def start_out_proj(first_conv_block):
        oa_s[...] = jnp.dot(_after(mixed_s[:, 0:d_a], first_conv_block), w_out_ref[0:d_a, :],
                            preferred_element_type=jnp.float32)

    u_bufs = (u1_s, u0_s)
    for j in range(n_chunks):
        cs = slice(j * MIX_COLS, (j + 1) * MIX_COLS)

        def emit_b(rs, lanes, acc):
            gc_s[rs, lanes] = acc

        conv_done = _causal_conv(extb_s, cbw_ref, cs, rows, HALO_B, cbb_ref, emit_b,
                                 on_first=start_out_proj if j == 0 else None)

        u_src, u_dst = u_bufs[j % 2], u_bufs[(j + 1) % 2]
        u_dst[...] = _after(u_src[...], conv_done)
        b_gate, c_gate, v_a, val_b, gate_b = (
            jnp.dot(u_dst[...], w_in_ref[:, q * d_a + cs.start:q * d_a + cs.stop],
                    preferred_element_type=jnp.float32) for q in range(N_IN_PIECES))
        extb_s[0:HALO_B, cs] = jnp.where(seq_start, 0.0, extb_s[rows:rows + HALO_B, cs])
        extb_s[HALO_B:HALO_B + rows, cs] = val_b * jax.nn.sigmoid(gate_b)
        exta_s[HALO_A:HALO_A + rows, cs] = c_gate * v_a

        def emit_a(rs, lanes, acc, b_gate=b_gate, c0=cs.start):
            gate = b_gate[rs, lanes.start - c0:lanes.stop - c0]
            mixed_s[rs, lanes] = (gate * acc).astype(jnp.bfloat16)

        _causal_conv(exta_s, caw_ref, cs, rows, HALO_A, None, emit_a)
        exta_s[0:HALO_A, cs] = exta_s[rows:rows + HALO_A, cs]

    for r0 in range(0, rows, NORM_ROWS):
        g = gc_s[r0:r0 + NORM_ROWS, :]
        gd = g - jnp.mean(g, axis=-1, keepdims=True)
        var = jnp.mean(gd * gd, axis=-1, keepdims=True)
        y = gd * lax.rsqrt(var + EPS) * lng_ref[...] + lnb_ref[...]
        mixed_s[r0:r0 + NORM_ROWS, d_a:d_a + d_b] = (
            y * jax.nn.sigmoid(y)).astype(jnp.bfloat16)
    o = oa_s[...] + jnp.dot(mixed_s[:, d_a:d_a + d_b], w_out_ref[d_a:d_a + d_b, :],
                            preferred_element_type=jnp.float32)
    o_ref[...] = xp_ref[...] + o * _rms_scale(o) * g_post_ref[...]


def _mlp_step(h_ref, w_up_ref, w_down_ref, o_ref, g_pre_ref, g_post_ref, m_s, acc_s, *, n_ff):
    j = pl.program_id(1)
    rows = m_s.shape[0]

    @pl.when(j == 0)
    def _():
        for r0 in range(0, rows, NORM_ROWS):
            hb = h_ref[r0:r0 + NORM_ROWS, :]
            m_s[r0:r0 + NORM_ROWS, :] = (hb * _rms_scale(hb) * g_pre_ref[...]).astype(jnp.bfloat16)

    z = jnp.dot(m_s[...], w_up_ref[...], preferred_element_type=jnp.float32)
    z = jnp.square(jnp.maximum(z, 0.0)).astype(jnp.bfloat16)
    part = jnp.dot(z, w_down_ref[...], preferred_element_type=jnp.float32)
    acc_s[...] = jnp.where(j == 0, part, acc_s[...] + part)

    @pl.when(j == n_ff - 1)
    def _():
        for r0 in range(0, rows, NORM_ROWS):
            zz = acc_s[r0:r0 + NORM_ROWS, :]
            o_ref[r0:r0 + NORM_ROWS, :] = (h_ref[r0:r0 + NORM_ROWS, :]
                                           + zz * _rms_scale(zz) * g_post_ref[...])


def _mlp_kernel(h_hbm, g_pre_ref, w_up_hbm, w_down_hbm, g_post_ref, o_hbm, m_s, acc_s,
                *, n_tiles, n_ff):
    d_model = h_hbm.shape[1]
    row_spec = pl.BlockSpec((MLP_ROWS, d_model), lambda i, j: (i, 0))
    pltpu.emit_pipeline(
        functools.partial(_mlp_step, n_ff=n_ff),
        grid=(n_tiles, n_ff),
        in_specs=[row_spec,
                  pl.BlockSpec((d_model, MLP_FF), lambda i, j: (0, j)),
                  pl.BlockSpec((MLP_FF, d_model), lambda i, j: (j, 0))],
        out_specs=[row_spec],
    )(h_hbm, w_up_hbm, w_down_hbm, o_hbm, scratches=(g_pre_ref, g_post_ref, m_s, acc_s))


def _resident(shape):
    return pl.BlockSpec(shape, lambda *_: (0,) * len(shape), pipeline_mode=pl.Buffered(1))


def _mixer(h2d, seq, g_pre, w_in, caw, cbw, cbb, lng, lnb, w_out, g_post, w_up, w_down):
    n_rows, d_model = h2d.shape
    d_a = caw.shape[1]
    d_b = cbw.shape[1]
    d_ff = w_up.shape[1]
    n_tiles = n_rows // MIX_ROWS
    last = lambda i: jnp.minimum(i, n_tiles - 1)
    cur_spec = pl.BlockSpec((MIX_ROWS, d_model), lambda i: (last(i), 0))
    prev_spec = pl.BlockSpec((MIX_ROWS, d_model), lambda i: (jnp.maximum(i - 1, 0), 0))
    w_up_spec = pl.BlockSpec((d_model // n_tiles, d_ff), lambda i: (last(i), 0))
    w_down_spec = pl.BlockSpec((d_ff // n_tiles, d_model), lambda i: (last(i), 0))
    return pl.pallas_call(
        functools.partial(_mixer_kernel, d_a=d_a, d_b=d_b, tiles_per_seq=seq // MIX_ROWS),
        out_shape=(jax.ShapeDtypeStruct((n_rows, d_model), jnp.float32),
                   jax.ShapeDtypeStruct(w_up.shape, jnp.bfloat16),
                   jax.ShapeDtypeStruct(w_down.shape, jnp.bfloat16)),
        grid=(n_tiles + 1,),
        in_specs=[cur_spec, prev_spec, _resident(g_pre.shape), _resident(w_in.shape),
                  _resident(caw.shape), _resident(cbw.shape), _resident(cbb.shape),
                  _resident(lng.shape), _resident(lnb.shape), _resident(w_out.shape),
                  _resident(g_post.shape), w_up_spec, w_down_spec],
        out_specs=(prev_spec, w_up_spec, w_down_spec),
        scratch_shapes=[
            pltpu.VMEM((MIX_ROWS, d_model), jnp.bfloat16),
            pltpu.VMEM((MIX_ROWS, d_model), jnp.bfloat16),
            pltpu.VMEM((HALO_A + MIX_ROWS, d_a), jnp.float32),
            pltpu.VMEM((HALO_B + MIX_ROWS, d_b), jnp.float32),
            pltpu.VMEM((MIX_ROWS, d_b), jnp.float32),
            pltpu.VMEM((MIX_ROWS, d_model), jnp.float32),
            pltpu.VMEM((MIX_ROWS, d_a + d_b), jnp.bfloat16),
        ],
        compiler_params=pltpu.CompilerParams(
            dimension_semantics=("arbitrary",),
            vmem_limit_bytes=VMEM_LIMIT_BYTES),
        name="mixer",
    )(h2d, h2d, g_pre, w_in, caw, cbw, cbb, lng, lnb, w_out, g_post, w_up, w_down)


def _mlp(h2d, g_pre, w_up, w_down, g_post):
    n_rows, d_model = h2d.shape
    d_ff = w_up.shape[1]
    hbm = pl.BlockSpec(memory_space=pl.ANY)
    vmem = pl.BlockSpec(memory_space=pltpu.VMEM)
    return pl.pallas_call(
        functools.partial(_mlp_kernel, n_tiles=n_rows // MLP_ROWS, n_ff=d_ff // MLP_FF),
        out_shape=jax.ShapeDtypeStruct((n_rows, d_model), jnp.float32),
        in_specs=[hbm, vmem, hbm, hbm, vmem],
        out_specs=hbm,
        scratch_shapes=[
            pltpu.VMEM((MLP_ROWS, d_model), jnp.bfloat16),
            pltpu.VMEM((MLP_ROWS, d_model), jnp.float32),
        ],
        compiler_params=pltpu.CompilerParams(vmem_limit_bytes=VMEM_LIMIT_BYTES),
        name="mlp",
    )(h2d, g_pre, w_up, w_down, g_post)


def kernel(x, mix_pre_gain, w_in, conv_a_w, conv_b_w, conv_b_bias, ln_b_gain, ln_b_bias,
           w_out, mix_post_gain, mlp_pre_gain, w_up, w_down, mlp_post_gain):
    batch, seq, d_model = x.shape
    depth = w_in.shape[0]
    d_a = conv_a_w.shape[2]
    d_b = conv_b_w.shape[2]
    assert d_a == d_b and w_in.shape[2] == N_IN_PIECES * d_a
    assert conv_a_w.shape[1] - 1 <= HALO_A and conv_b_w.shape[1] - 1 <= HALO_B
    assert d_a % MIX_COLS == 0 and seq % MIX_ROWS == 0 and MIX_ROWS >= HALO_B
    assert MIX_ROWS % CONV_ROWS == 0 and MIX_ROWS % NORM_ROWS == 0
    assert (batch * seq) % MLP_ROWS == 0 and w_up.shape[2] % MLP_FF == 0
    assert MLP_ROWS % NORM_ROWS == 0
    n_mix_tiles = batch * seq // MIX_ROWS
    assert d_model % (2 * SUBLANES * n_mix_tiles) == 0
    assert w_up.shape[2] % (2 * SUBLANES * n_mix_tiles) == 0
    bf16 = jnp.bfloat16
    row = lambda a: a.reshape(1, -1)

    h = x.reshape(batch * seq, d_model)
    for l in range(depth):
        h, w_up_bf16, w_down_bf16 = _mixer(
            h, seq, row(mix_pre_gain[l]), w_in[l].astype(bf16),
            conv_a_w[l], conv_b_w[l], row(conv_b_bias[l]),
            row(ln_b_gain[l]), row(ln_b_bias[l]),
            w_out[l].astype(bf16), row(mix_post_gain[l]), w_up[l], w_down[l])
        h = _mlp(h, row(mlp_pre_gain[l]), w_up_bf16, w_down_bf16, row(mlp_post_gain[l]))
    return h.reshape(batch, seq, d_model)
```
